```python
import math
import jax, jax.numpy as jnp
from jax import lax
import numpy as np

D_MODEL = 2048
BATCH = 2
SEQ = 16384
DEPTH = 4

GRID_W = 64
CTX_LEN = 256
D_FOURIER = D_MODEL // 2
N_FOURIER_GROUPS = 4
D_ATTN = D_MODEL // 2
N_HEADS = 8
V_DIM = D_ATTN // N_HEADS
HEAD_DIM = V_DIM // 2
D_QK = N_HEADS * 2 * HEAD_DIM
ROPE_THETA = 10000.0
Q_BLOCK = 128
D_LRU = D_MODEL
N_LRU_BLOCKS = 8
LRU_BLOCK = D_LRU // N_LRU_BLOCKS
CONV_W = 4
CONV_PAD = (2, 1)
LRU_C = 8.0
D_FF = ((8 * D_MODEL + 3 * 256 - 1) // (3 * 256)) * 256
EPS = 1e-6
F32 = jnp.float32

kernel_name = 'hybrid_fourier_diffattn_rglru_prefix_dit'


def rms_norm(x, g):
    xf = x.astype(F32)
    y = xf * lax.rsqrt(jnp.mean(xf * xf, axis=-1, keepdims=True) + EPS)
    return (y * g.astype(F32)).astype(x.dtype)


def adaln(cond, w, b):
    return jnp.split(jax.nn.silu(cond) @ w + b, 6, axis=-1)


def modulate(h, shift, scale):
    return h * (1 + scale) + shift


def swiglu(h, w_gate, w_up, w_down):
    return (jax.nn.silu(h @ w_gate) * (h @ w_up)) @ w_down


def axial_rope_tables(n):
    n_rows = n // GRID_W
    rows = jnp.repeat(jnp.arange(n_rows, dtype=F32), GRID_W, total_repeat_length=n)
    cols = jnp.tile(jnp.arange(GRID_W, dtype=F32), n_rows)
    n_pairs_axis = HEAD_DIM // 4
    inv = ROPE_THETA ** (-jnp.arange(n_pairs_axis, dtype=F32) / n_pairs_axis)
    ang = jnp.concatenate([rows[:, None] * inv, cols[:, None] * inv], axis=-1)
    return jnp.cos(ang), jnp.sin(ang)


def apply_rope(x, cos, sin):
    xf = x.astype(F32)
    x1, x2 = xf[..., 0::2], xf[..., 1::2]
    c = cos[None, :, None, None, :]
    s = sin[None, :, None, None, :]
    y = jnp.stack([x1 * c - x2 * s, x1 * s + x2 * c], axis=-1).reshape(x.shape)
    return y.astype(x.dtype)


def fourier_mix(u):
    b, l, _ = u.shape
    ug = u.astype(F32).reshape(b, l, N_FOURIER_GROUPS, D_FOURIER // N_FOURIER_GROUPS)
    y = jnp.fft.fft2(ug, axes=(1, 3), norm='ortho').real
    return y.reshape(b, l, D_FOURIER).astype(u.dtype)


def diff_attend(q, k, v, lam, lam_init, head_g):
    s = jnp.einsum('bqhmd,bkhmd->bhmqk', q, k).astype(F32) * (HEAD_DIM ** -0.5)
    p = jax.nn.softmax(s, axis=-1)
    a = p[:, :, 0] - lam * p[:, :, 1]
    o = jnp.einsum('bhqk,bkhe->bqhe', a, v.astype(F32))
    o = o * lax.rsqrt(jnp.mean(o * o, axis=-1, keepdims=True) + EPS)
    o = o * head_g.astype(F32) * (1.0 - lam_init)
    return o.astype(q.dtype)


def qk_heads(u, g):
    b, l, _ = u.shape
    return rms_norm(u.reshape(b, l, N_HEADS, 2, HEAD_DIM), g)


def mixer_fourier_diffattn(hx, hc, w_in, w_out, q_g, k_g, lam_q1, lam_k1, lam_q2, lam_k2,
                           head_g, lam_init, ctx_out):
    b, s, _ = hx.shape
    lc = hc.shape[1]
    ux = hx @ w_in
    fx = ux[..., :D_FOURIER]
    qx = qk_heads(ux[..., D_FOURIER:D_FOURIER + D_QK], q_g)
    kx = qk_heads(ux[..., D_FOURIER + D_QK:D_FOURIER + 2 * D_QK], k_g)
    vx = ux[..., D_FOURIER + 2 * D_QK:].reshape(b, s, N_HEADS, V_DIM)
    cos, sin = axial_rope_tables(s)
    qx = apply_rope(qx, cos, sin)
    kx = apply_rope(kx, cos, sin)
    if ctx_out:
        uc = hc @ w_in
        fc = uc[..., :D_FOURIER]
        qc = qk_heads(uc[..., D_FOURIER:D_FOURIER + D_QK], q_g)
        kvc = uc[..., D_FOURIER + D_QK:]
    else:
        kvc = hc @ w_in[:, D_FOURIER + D_QK:]
    kc = qk_heads(kvc[..., :D_QK], k_g)
    vc = kvc[..., D_QK:].reshape(b, lc, N_HEADS, V_DIM)
    lam = (jnp.exp(jnp.sum(lam_q1.astype(F32) * lam_k1.astype(F32)))
           - jnp.exp(jnp.sum(lam_q2.astype(F32) * lam_k2.astype(F32))) + lam_init)
    k_all = jnp.concatenate([kc, kx], axis=1)
    v_all = jnp.concatenate([vc, vx], axis=1)
    nb = s // Q_BLOCK
    q_blocks = jnp.moveaxis(qx.reshape(b, nb, Q_BLOCK, N_HEADS, 2, HEAD_DIM), 1, 0)
    o = lax.map(lambda qb: diff_attend(qb, k_all, v_all, lam, lam_init, head_g), q_blocks)
    ox = jnp.moveaxis(o, 0, 1).reshape(b, s, D_ATTN)
    yx = jnp.concatenate([fourier_mix(fx), ox], axis=-1) @ w_out
    if not ctx_out:
        return yx, None
    oc = diff_attend(qc, kc, vc, lam, lam_init, head_g).reshape(b, lc, D_ATTN)
    yc = jnp.concatenate([fourier_mix(fc), oc], axis=-1) @ w_out
    return yx, yc


def dw_conv(x, w, bias):
    y = lax.conv_general_dilated(x, w[:, None, :].astype(x.dtype), window_strides=(1,),
                                 padding=[CONV_PAD], dimension_numbers=('NWC', 'WIO', 'NWC'),
                                 feature_group_count=x.shape[-1])
    return y + bias


def block_diag(x, w, bias):
    b, l, _ = x.shape
    y = jnp.einsum('blnc,ncd->blnd', x.reshape(b, l, N_LRU_BLOCKS, LRU_BLOCK), w)
    return y.reshape(b, l, D_LRU) + bias


def lru_coeffs(xc, wa, ba, wx, bx, lam):
    r = jax.nn.sigmoid(block_diag(xc, wa, ba).astype(F32))
    i = jax.nn.sigmoid(block_diag(xc, wx, bx).astype(F32))
    log_a = -LRU_C * r * jax.nn.softplus(-lam.astype(F32))
    a = jnp.exp(log_a)
    mult = jnp.sqrt(-jnp.expm1(2.0 * log_a))
    return a, mult * i * xc.astype(F32)


def linear_scan(a, bx, h0, reverse):
    def combine(e1, e2):
        a1, b1 = e1
        a2, b2 = e2
        return a1 * a2, a2 * b1 + b2
    a_cum, b_cum = lax.associative_scan(combine, (a, bx), axis=1, reverse=reverse)
    if h0 is None:
        return b_cum
    return b_cum + a_cum * h0[:, None, :]


def mixer_rglru(hx, hc, w_in, w_out, conv_w, conv_b, ga_w, ga_b, gx_w, gx_b, lru_lam, ctx_out):
    ux = hx @ w_in
    gate_x, rec_x = ux[..., :D_LRU], ux[..., D_LRU:]
    if ctx_out:
        uc = hc @ w_in
        gate_c, rec_c = uc[..., :D_LRU], uc[..., D_LRU:]
    else:
        rec_c = hc @ w_in[:, D_LRU:]
    xc_c = dw_conv(rec_c, conv_w, conv_b)
    xc_x = dw_conv(rec_x, conv_w, conv_b)
    y_x = 0.0
    y_c = 0.0
    for d, rev in enumerate((False, True)):
        a_c, b_c = lru_coeffs(xc_c, ga_w[d], ga_b[d], gx_w[d], gx_b[d], lru_lam[d])
        h_c = linear_scan(a_c, b_c, None, rev)
        h_final = h_c[:, 0] if rev else h_c[:, -1]
        a_x, b_x = lru_coeffs(xc_x, ga_w[d], ga_b[d], gx_w[d], gx_b[d], lru_lam[d])
        y_x = y_x + linear_scan(a_x, b_x, h_final, rev)
        if ctx_out:
            y_c = y_c + h_c
    yx = (y_x * jax.nn.gelu(gate_x.astype(F32))).astype(hx.dtype) @ w_out
    if not ctx_out:
        return yx, None
    yc = (y_c * jax.nn.gelu(gate_c.astype(F32))).astype(hc.dtype) @ w_out
    return yx, yc


def setup_inputs(seed: int = 0) -> dict:
    key = jax.random.key(seed)
    ks = iter(jax.random.split(key, 40))
    n_ab = (DEPTH + 1) // 2
    n_c = DEPTH // 2

    def nrm(shape, fan_in, gain=1.0):
        return jax.random.normal(next(ks), shape, F32) * (gain * fan_in ** -0.5)

    def small(shape, s=0.01):
        return jax.random.normal(next(ks), shape, F32) * s

    def ones_noise(shape):
        return 1.0 + small(shape)

    u = jax.random.uniform(next(ks), (n_c, 2, D_LRU), F32, minval=0.9, maxval=0.999)
    a0 = u ** (1.0 / LRU_C)
    lru_lambda = jnp.log(a0) - jnp.log1p(-a0)
    return {
        'x': jax.random.normal(next(ks), (BATCH, SEQ, D_MODEL), F32),
        'c': jax.random.normal(next(ks), (BATCH, D_MODEL), F32),
        'ctx': jax.random.normal(next(ks), (BATCH, CTX_LEN, D_MODEL), F32),
        'c_ctx': jax.random.normal(next(ks), (D_MODEL,), F32),
        'ada_w': nrm((DEPTH, D_MODEL, 6 * D_MODEL), D_MODEL, 0.5),
        'ada_b': small((DEPTH, 6 * D_MODEL)),
        'norm_mix_g': ones_noise((DEPTH, D_MODEL)),
        'norm_ffn_g': ones_noise((DEPTH, D_MODEL)),
        'ffn_w_gate': nrm((DEPTH, D_MODEL, D_FF), D_MODEL),
        'ffn_w_up': nrm((DEPTH, D_MODEL, D_FF), D_MODEL),
        'ffn_w_down': nrm((DEPTH, D_FF, D_MODEL), D_FF),
        'ab_w_in': nrm((n_ab, D_MODEL, D_FOURIER + 2 * D_QK + D_ATTN), D_MODEL),
        'ab_w_out': nrm((n_ab, D_FOURIER + D_ATTN, D_MODEL), D_FOURIER + D_ATTN),
        'ab_q_norm_g': ones_noise((n_ab, HEAD_DIM)),
        'ab_k_norm_g': ones_noise((n_ab, HEAD_DIM)),
        'ab_lam_q1': small((n_ab, HEAD_DIM), 0.1),
        'ab_lam_k1': small((n_ab, HEAD_DIM), 0.1),
        'ab_lam_q2': small((n_ab, HEAD_DIM), 0.1),
        'ab_lam_k2': small((n_ab, HEAD_DIM), 0.1),
        'ab_head_norm_g': ones_noise((n_ab, V_DIM)),
        'lru_w_in': nrm((n_c, D_MODEL, 2 * D_LRU), D_MODEL),
        'lru_w_out': nrm((n_c, D_LRU, D_MODEL), D_LRU),
        'lru_conv_w': nrm((n_c, CONV_W, D_LRU), CONV_W),
        'lru_conv_b': small((n_c, D_LRU)),
        'lru_gate_a_w': nrm((n_c, 2, N_LRU_BLOCKS, LRU_BLOCK, LRU_BLOCK), LRU_BLOCK),
        'lru_gate_a_b': small((n_c, 2, D_LRU)),
        'lru_gate_x_w': nrm((n_c, 2, N_LRU_BLOCKS, LRU_BLOCK, LRU_BLOCK), LRU_BLOCK),
        'lru_gate_x_b': small((n_c, 2, D_LRU)),
        'lru_lambda': lru_lambda,
    }


def reference(x, c, ctx, c_ctx, ada_w, ada_b, norm_mix_g, norm_ffn_g, ffn_w_gate, ffn_w_up,
              ffn_w_down, ab_w_in, ab_w_out, ab_q_norm_g, ab_k_norm_g, ab_lam_q1, ab_lam_k1,
              ab_lam_q2, ab_lam_k2, ab_head_norm_g, lru_w_in, lru_w_out, lru_conv_w, lru_conv_b,
              lru_gate_a_w, lru_gate_a_b, lru_gate_x_w, lru_gate_x_b, lru_lambda):
    for l in range(DEPTH):
        last = l == DEPTH - 1
        j = l // 2
        mx = [m[:, None, :] for m in adaln(c, ada_w[l], ada_b[l])]
        mc = adaln(c_ctx, ada_w[l], ada_b[l])
        hx = modulate(rms_norm(x, norm_mix_g[l]), mx[0], mx[1])
        hc = modulate(rms_norm(ctx, norm_mix_g[l]), mc[0], mc[1])
        if l % 2 == 0:
            lam_init = 0.8 - 0.6 * math.exp(-0.3 * l)
            yx, yc = mixer_fourier_diffattn(hx, hc, ab_w_in[j], ab_w_out[j], ab_q_norm_g[j],
                                            ab_k_norm_g[j], ab_lam_q1[j], ab_lam_k1[j],
                                            ab_lam_q2[j], ab_lam_k2[j], ab_head_norm_g[j],
                                            lam_init, not last)
        else:
            yx, yc = mixer_rglru(hx, hc, lru_w_in[j], lru_w_out[j], lru_conv_w[j], lru_conv_b[j],
                                 lru_gate_a_w[j], lru_gate_a_b[j], lru_gate_x_w[j],
                                 lru_gate_x_b[j], lru_lambda[j], not last)
        x = x + mx[2] * yx
        hx = modulate(rms_norm(x, norm_ffn_g[l]), mx[3], mx[4])
        x = x + mx[5] * swiglu(hx, ffn_w_gate[l], ffn_w_up[l], ffn_w_down[l])
        if not last:
            ctx = ctx + mc[2] * yc
            hc = modulate(rms_norm(ctx, norm_ffn_g[l]), mc[3], mc[4])
            ctx = ctx + mc[5] * swiglu(hc, ffn_w_gate[l], ffn_w_up[l], ffn_w_down[l])
    return x
```

```python
import functools
import math

import jax
import jax.numpy as jnp
import numpy as np
from jax import lax
from jax.experimental import pallas as pl
from jax.experimental.pallas import tpu as pltpu

F32 = jnp.float32
BF16 = jnp.bfloat16

EPS = 1e-6
GRID_W = 64
ROPE_THETA = 10000.0
HEAD_DIM = 64
V_DIM = 2 * HEAD_DIM
N_FOURIER_GROUPS = 4
LRU_C = 8.0
LOG2E = 1.4426950408889634

V7X_LANES = 128
V7X_BF16_SUBLANES = 16
V7X_VMEM_BYTES = 64 * 1024 * 1024
VMEM_LIMIT = V7X_VMEM_BYTES - 8 * 1024 * 1024

FOURIER_SLOW = 32
NEG_BIG = -1e30


def _cparams(sem):
    return pltpu.CompilerParams(dimension_semantics=sem, vmem_limit_bytes=VMEM_LIMIT)


def _pick_tile(n, target, mult):
    best = None
    for t in range(mult, min(n, target) + 1, mult):
        if n % t == 0:
            best = t
    assert best is not None, (n, target, mult)
    return best


def _row_mods(mod_ref, row0, rows, n_ctx, idx):
    r = row0 + lax.broadcasted_iota(jnp.int32, (rows, 1), 0)
    is_ctx = r < n_ctx
    return [jnp.where(is_ctx, mod_ref[0, 0, k:k + 1, :], mod_ref[0, 1, k:k + 1, :]) for k in idx]


def _norm_mod(x, g, shift, scale):
    y = x * lax.rsqrt(jnp.mean(x * x, axis=-1, keepdims=True) + EPS)
    return (y * g) * (1.0 + scale) + shift


def _silu(x):
    return x * jax.nn.sigmoid(x)


def _gelu_tanh(x):
    return 0.5 * x * (1.0 + jnp.tanh(math.sqrt(2.0 / math.pi) * (x + 0.044715 * (x * x * x))))


def _adaln_kernel(cond_ref, w_ref, b_ref, o_ref):
    s = _silu(cond_ref[...])
    o_ref[0] = jnp.dot(s.astype(BF16), w_ref[0].astype(BF16),
                       preferred_element_type=F32) + b_ref[0]


def _adaln(cond8, ada_w, ada_b):
    depth, d, n6 = ada_w.shape
    tn = _pick_tile(n6, 1024, V7X_LANES)
    return pl.pallas_call(
        _adaln_kernel,
        grid=(depth, n6 // tn),
        in_specs=[pl.BlockSpec((8, d), lambda l, j: (0, 0)),
                  pl.BlockSpec((1, d, tn), lambda l, j: (l, 0, j)),
                  pl.BlockSpec((1, 1, tn), lambda l, j: (l, 0, j))],
        out_specs=pl.BlockSpec((1, 8, tn), lambda l, j: (l, 0, j)),
        out_shape=jax.ShapeDtypeStruct((depth, 8, n6), F32),
        compiler_params=_cparams(("parallel", "parallel")),
        name="adaln",
    )(cond8, ada_w, ada_b.reshape(depth, 1, n6))


def _proj_kernel(x_ref, mod_ref, g_ref, w_ref, o_ref, h_scr, *, n_ctx, tm):
    row0 = pl.program_id(1) * tm

    @pl.when(pl.program_id(2) == 0)
    def _():
        shift, scale = _row_mods(mod_ref, row0, tm, n_ctx, (0, 1))
        h_scr[...] = _norm_mod(x_ref[0], g_ref[...], shift, scale).astype(BF16)

    o_ref[0] = jnp.dot(h_scr[...], w_ref[...], preferred_element_type=F32).astype(o_ref.dtype)


def _proj(xc, mods, g, w, n_ctx):
    b, n, d = xc.shape
    nout = w.shape[1]
    tm = _pick_tile(n, 640, 128)
    tn = _pick_tile(nout, 1024, V7X_LANES)
    return pl.pallas_call(
        functools.partial(_proj_kernel, n_ctx=n_ctx, tm=tm),
        grid=(b, n // tm, nout // tn),
        in_specs=[pl.BlockSpec((1, tm, d), lambda bi, i, j: (bi, i, 0)),
                  pl.BlockSpec((1, 2, 6, d), lambda bi, i, j: (bi, 0, 0, 0)),
                  pl.BlockSpec((1, d), lambda bi, i, j: (0, 0)),
                  pl.BlockSpec((d, tn), lambda bi, i, j: (0, j))],
        out_specs=pl.BlockSpec((1, tm, tn), lambda bi, i, j: (bi, i, j)),
        out_shape=jax.ShapeDtypeStruct((b, n, nout), BF16),
        scratch_shapes=[pltpu.VMEM((tm, d), BF16)],
        compiler_params=_cparams(("parallel", "parallel", "arbitrary")),
        name="proj",
    )(xc, mods, g.reshape(1, d), w)


def _qk_prep_kernel(q_ref, k_ref, cos_ref, sin_ref, gq_ref, gk_ref, qo_ref, ko_ref, *, heads, q_scale):
    cos = cos_ref[...]
    sin = sin_ref[...]
    for src, g_ref, dst, sc in ((q_ref, gq_ref, qo_ref, q_scale), (k_ref, gk_ref, ko_ref, 1.0)):
        g = g_ref[...]
        for h in range(heads):
            x = src[0, :, h * V_DIM:(h + 1) * V_DIM].astype(F32)
            t = x * x
            for sh in (64, 32, 16, 8, 4, 2):
                t = t + pltpu.roll(t, sh, axis=1)
            y = x * lax.rsqrt(t * (1.0 / HEAD_DIM) + EPS) * g
            y = y * cos + pltpu.roll(y, 64, axis=1) * sin
            dst[0, :, h * V_DIM:(h + 1) * V_DIM] = (y * sc).astype(BF16)


def _qk_prep(u, cos_t, sin_t, gq, gk, heads, q_col, k_col):
    b, n, _ = u.shape
    dq = heads * V_DIM
    tm = _pick_tile(n, 640, 128)
    q_scale = (HEAD_DIM ** -0.5) * LOG2E
    out = jax.ShapeDtypeStruct((b, n, dq), BF16)
    return pl.pallas_call(
        functools.partial(_qk_prep_kernel, heads=heads, q_scale=q_scale),
        grid=(b, n // tm),
        in_specs=[pl.BlockSpec((1, tm, dq), lambda bi, i: (bi, i, q_col)),
                  pl.BlockSpec((1, tm, dq), lambda bi, i: (bi, i, k_col)),
                  pl.BlockSpec((tm, V_DIM), lambda bi, i: (i, 0)),
                  pl.BlockSpec((tm, V_DIM), lambda bi, i: (i, 0)),
                  pl.BlockSpec((1, V_DIM), lambda bi, i: (0, 0)),
                  pl.BlockSpec((1, V_DIM), lambda bi, i: (0, 0))],
        out_specs=[pl.BlockSpec((1, tm, dq), lambda bi, i: (bi, i, 0)),
                   pl.BlockSpec((1, tm, dq), lambda bi, i: (bi, i, 0))],
        out_shape=[out, out],
        compiler_params=_cparams(("parallel", "parallel")),
        name="qk_prep",
    )(u, u, cos_t, sin_t, gq, gk)


def _attn_kernel(lamv_ref, hg_ref, q_ref, k_ref, vt_ref, o_ref, m_scr, l_scr, acc_scr,
                 *, n_ctx, tk, n_lat_chunks, lam_init):
    qi = pl.program_id(2)
    q = q_ref[0]
    lane = lax.broadcasted_iota(jnp.int32, q.shape, 1)
    even = (lane % 2) == 0
    zero = jnp.zeros_like(q)
    qm = (jnp.where(even, q, zero), jnp.where(even, zero, q))

    m_scr[...] = jnp.full(m_scr.shape, NEG_BIG, F32)
    l_scr[...] = jnp.zeros(l_scr.shape, F32)
    acc_scr[...] = jnp.zeros(acc_scr.shape, F32)

    def process(kc, vtc):
        for mp in range(2):
            s = lax.dot_general(kc, qm[mp], (((1,), (1,)), ((), ())), preferred_element_type=F32)
            m_old = m_scr[mp]
            m_new = jnp.maximum(m_old, jnp.max(s, axis=0, keepdims=True))
            alpha = jnp.exp2(m_old - m_new)
            p = jnp.exp2(s - m_new)
            l_scr[mp] = alpha * l_scr[mp] + jnp.sum(p, axis=0, keepdims=True)
            acc_scr[mp] = alpha * acc_scr[mp] + jnp.dot(vtc, p.astype(BF16), preferred_element_type=F32)
            m_scr[mp] = m_new

    process(k_ref[0, 0:n_ctx, :], vt_ref[0, 0, 0])

    units = tk // n_ctx

    def body(c, carry):
        start = pl.multiple_of(n_ctx + c * tk, n_ctx)
        kc = k_ref[0, pl.ds(start, tk), :]
        vtc = jnp.concatenate([vt_ref[0, 0, 1 + c * units + u] for u in range(units)], axis=1)
        process(kc, vtc)
        return carry

    lax.fori_loop(0, jnp.where(qi == 0, 0, n_lat_chunks), body, 0)

    lv = lamv_ref[...]
    lam = (jnp.exp(jnp.sum(lv[0:1] * lv[1:2], axis=1, keepdims=True))
           - jnp.exp(jnp.sum(lv[2:3] * lv[3:4], axis=1, keepdims=True)) + lam_init)
    ot = acc_scr[0] / l_scr[0] - lam * (acc_scr[1] / l_scr[1])
    o = ot.T
    o = o * lax.rsqrt(jnp.mean(o * o, axis=-1, keepdims=True) + EPS)
    o_ref[0] = (o * hg_ref[...] * (1.0 - lam_init)).astype(o_ref.dtype)


def _attention(lamv, head_g, q, k, vt, n_ctx, lam_init):
    b, n, dq = q.shape
    heads = dq // V_DIM
    tq = n_ctx
    s = n - n_ctx
    tk = _pick_tile(s, 1024, n_ctx)
    return pl.pallas_call(
        functools.partial(_attn_kernel, n_ctx=n_ctx, tk=tk, n_lat_chunks=s // tk, lam_init=lam_init),
        grid=(b, heads, n // tq),
        in_specs=[pl.BlockSpec((4, HEAD_DIM), lambda bi, h, i: (0, 0)),
                  pl.BlockSpec((1, V_DIM), lambda bi, h, i: (0, 0)),
                  pl.BlockSpec((1, tq, V_DIM), lambda bi, h, i: (bi, i, h)),
                  pl.BlockSpec((1, n, V_DIM), lambda bi, h, i: (bi, 0, h)),
                  pl.BlockSpec((1, 1, n // n_ctx, V_DIM, n_ctx), lambda bi, h, i: (bi, h, 0, 0, 0))],
        out_specs=pl.BlockSpec((1, tq, V_DIM), lambda bi, h, i: (bi, i, h)),
        out_shape=jax.ShapeDtypeStruct((b, n, dq), BF16),
        scratch_shapes=[pltpu.VMEM((2, 1, tq), F32), pltpu.VMEM((2, 1, tq), F32),
                        pltpu.VMEM((2, V_DIM, tq), F32)],
        compiler_params=_cparams(("parallel", "parallel", "arbitrary")),
        name="diff_attn",
    )(lamv, head_g, q, k, vt)


def _dft_cos_sin(n, rows=None, cols=None, mod=None):
    mod = n if mod is None else mod
    r = jnp.arange(n if rows is None else rows, dtype=jnp.int32)[:, None]
    c = jnp.arange(n if cols is None else cols, dtype=jnp.int32)[None, :]
    ang = ((r * c) % mod).astype(F32) * (2.0 * math.pi / mod)
    return jnp.cos(ang), jnp.sin(ang)


def _channel_dft(x, cs_ref, groups):
    cg = x.shape[1] // groups
    wr, wi = [], []
    for g in range(groups):
        w = jnp.dot(x[:, g * cg:(g + 1) * cg], cs_ref[...], preferred_element_type=F32)
        wr.append(w[:, :cg])
        wi.append(w[:, cg:])
    return jnp.concatenate(wr, axis=1), jnp.concatenate(wi, axis=1)


def _fourier1_kernel(x_ref, cs_ref, kc_ref, ks_ref, tc_ref, ts_ref, zr_ref, zi_ref, *, groups):
    ls, tf, c = x_ref.shape[1:]
    x = x_ref[0].reshape(ls * tf, c)
    wr, wi = _channel_dft(x, cs_ref, groups)
    wr = wr.astype(BF16)
    wi = wi.astype(BF16)
    kc = kc_ref[...]
    ks = ks_ref[...]
    zr = jnp.dot(kc, wr, preferred_element_type=F32) + jnp.dot(ks, wi, preferred_element_type=F32)
    zi = jnp.dot(kc, wi, preferred_element_type=F32) - jnp.dot(ks, wr, preferred_element_type=F32)
    reps = c // V7X_LANES
    tc = jnp.concatenate([tc_ref[0]] * reps, axis=1)
    ts = jnp.concatenate([ts_ref[0]] * reps, axis=1)
    zr_ref[0] = (zr * tc + zi * ts).astype(BF16).reshape(ls, tf, c)
    zi_ref[0] = (zi * tc - zr * ts).astype(BF16).reshape(ls, tf, c)


def _fourier2_kernel(zr_ref, zi_ref, fc_ref, fs_ref, y_ref, *, scale):
    for u in range(zr_ref.shape[1]):
        y = (jnp.dot(fc_ref[...], zr_ref[0, u], preferred_element_type=F32)
             + jnp.dot(fs_ref[...], zi_ref[0, u], preferred_element_type=F32))
        y_ref[0, u] = (y * scale).astype(BF16)


def _fourier_ctx_kernel(x_ref, cs_ref, pc_ref, ps_ref, y_ref, *, groups, scale):
    wr, wi = _channel_dft(x_ref[0], cs_ref, groups)
    y = (jnp.dot(pc_ref[...], wr.astype(BF16), preferred_element_type=F32)
         + jnp.dot(ps_ref[...], wi.astype(BF16), preferred_element_type=F32))
    y_ref[0] = (y * scale).astype(BF16)


def _fourier_tables(s, c):
    cg = c // N_FOURIER_GROUPS
    cc, sc = _dft_cos_sin(cg)
    cs = jnp.concatenate([cc, -sc], axis=1).astype(BF16)
    return cg, cs


def _fourier_latent(fx, groups):
    b, s, c = fx.shape
    cg, cs = _fourier_tables(s, c)
    ls = FOURIER_SLOW
    lf = s // ls
    tf = V7X_BF16_SUBLANES
    assert ls * lf == s and lf % tf == 0
    rows = ls * tf
    pc, ps = _dft_cos_sin(ls)
    eye = jnp.eye(tf, dtype=F32)
    kc = jnp.kron(pc, eye).astype(BF16)
    ks = jnp.kron(ps, eye).astype(BF16)
    u_idx = jnp.repeat(jnp.arange(ls, dtype=jnp.int32), tf)[None, :]
    f_idx = (jnp.arange(lf // tf, dtype=jnp.int32)[:, None] * tf
             + jnp.tile(jnp.arange(tf, dtype=jnp.int32), ls)[None, :])
    ang = ((u_idx * f_idx) % s).astype(F32) * (2.0 * math.pi / s)
    tc = jnp.broadcast_to(jnp.cos(ang)[:, :, None], (lf // tf, rows, V7X_LANES))
    ts = jnp.broadcast_to(jnp.sin(ang)[:, :, None], (lf // tf, rows, V7X_LANES))

    x4 = fx.reshape(b, ls, lf, c)
    zshape = jax.ShapeDtypeStruct((b, ls, lf, c), BF16)
    blk = pl.BlockSpec((1, ls, tf, c), lambda bi, i: (bi, 0, i, 0))
    zr, zi = pl.pallas_call(
        functools.partial(_fourier1_kernel, groups=groups),
        grid=(b, lf // tf),
        in_specs=[blk,
                  pl.BlockSpec((cg, 2 * cg), lambda bi, i: (0, 0)),
                  pl.BlockSpec((rows, rows), lambda bi, i: (0, 0)),
                  pl.BlockSpec((rows, rows), lambda bi, i: (0, 0)),
                  pl.BlockSpec((1, rows, V7X_LANES), lambda bi, i: (i, 0, 0)),
                  pl.BlockSpec((1, rows, V7X_LANES), lambda bi, i: (i, 0, 0))],
        out_specs=[blk, blk],
        out_shape=[zshape, zshape],
        compiler_params=_cparams(("parallel", "parallel")),
        name="fourier_slow",
    )(x4, cs, kc, ks, tc, ts)

    fc, fs = _dft_cos_sin(lf)
    tu = _pick_tile(ls, 4, 1)
    blk2 = pl.BlockSpec((1, tu, lf, c), lambda bi, i: (bi, i, 0, 0))
    yt = pl.pallas_call(
        functools.partial(_fourier2_kernel, scale=1.0 / math.sqrt(s * cg)),
        grid=(b, ls // tu),
        in_specs=[blk2, blk2,
                  pl.BlockSpec((lf, lf), lambda bi, i: (0, 0)),
                  pl.BlockSpec((lf, lf), lambda bi, i: (0, 0))],
        out_specs=blk2,
        out_shape=zshape,
        compiler_params=_cparams(("parallel", "parallel")),
        name="fourier_fast",
    )(zr, zi, fc.astype(BF16), fs.astype(BF16))
    return jnp.swapaxes(yt, 1, 2).reshape(b, s, c)


def _fourier_ctx(fxc, groups):
    b, n_ctx, c = fxc.shape
    cg, cs = _fourier_tables(n_ctx, c)
    pc, ps = _dft_cos_sin(n_ctx)
    return pl.pallas_call(
        functools.partial(_fourier_ctx_kernel, groups=groups, scale=1.0 / math.sqrt(n_ctx * cg)),
        grid=(b,),
        in_specs=[pl.BlockSpec((1, n_ctx, c), lambda bi: (bi, 0, 0)),
                  pl.BlockSpec((cg, 2 * cg), lambda bi: (0, 0)),
                  pl.BlockSpec((n_ctx, n_ctx), lambda bi: (0, 0)),
                  pl.BlockSpec((n_ctx, n_ctx), lambda bi: (0, 0))],
        out_specs=pl.BlockSpec((1, n_ctx, c), lambda bi: (bi, 0, 0)),
        out_shape=jax.ShapeDtypeStruct((b, n_ctx, c), BF16),
        compiler_params=_cparams(("parallel",)),
        name="fourier_ctx",
    )(fxc, cs, pc.astype(BF16), ps.astype(BF16))


def _out_ab_kernel(x_ref, f_ref, o_ref_in, mod_ref, w_ref, out_ref, *, n_ctx, tm, c):
    (gate,) = _row_mods(mod_ref, pl.program_id(1) * tm, tm, n_ctx, (2,))
    y = (jnp.dot(f_ref[0], w_ref[0:c, :], preferred_element_type=F32)
         + jnp.dot(o_ref_in[0], w_ref[c:, :], preferred_element_type=F32))
    out_ref[0] = x_ref[0] + gate * y


def _out_ab(xc, f, o, mods, w_out, n_ctx):
    b, n, d = xc.shape
    c = f.shape[2]
    tm = _pick_tile(n, 640, 128)
    return pl.pallas_call(
        functools.partial(_out_ab_kernel, n_ctx=n_ctx, tm=tm, c=c),
        grid=(b, n // tm),
        in_specs=[pl.BlockSpec((1, tm, d), lambda bi, i: (bi, i, 0)),
                  pl.BlockSpec((1, tm, c), lambda bi, i: (bi, i, 0)),
                  pl.BlockSpec((1, tm, o.shape[2]), lambda bi, i: (bi, i, 0)),
                  pl.BlockSpec((1, 2, 6, d), lambda bi, i: (bi, 0, 0, 0)),
                  pl.BlockSpec(w_out.shape, lambda bi, i: (0, 0))],
        out_specs=pl.BlockSpec((1, tm, d), lambda bi, i: (bi, i, 0)),
        out_shape=jax.ShapeDtypeStruct((b, n, d), F32),
        compiler_params=_cparams(("parallel", "parallel")),
        name="out_proj_attn",
    )(xc, f, o, mods, w_out)


def _out_lru_kernel(x_ref, yf_ref, yr_ref, gate_ref, mod_ref, w_ref, out_ref, *, n_ctx, tm):
    (gate,) = _row_mods(mod_ref, pl.program_id(1) * tm, tm, n_ctx, (2,))
    z = (yf_ref[0] + yr_ref[0]) * _gelu_tanh(gate_ref[0].astype(F32))
    y = jnp.dot(z.astype(BF16), w_ref[...], preferred_element_type=F32)
    out_ref[0] = x_ref[0] + gate * y


def _out_lru(xc, yf, yr, u, mods, w_out, n_ctx):
    b, n, d = xc.shape
    tm = _pick_tile(n, 320, 64)
    row = pl.BlockSpec((1, tm, d), lambda bi, i: (bi, i, 0))
    return pl.pallas_call(
        functools.partial(_out_lru_kernel, n_ctx=n_ctx, tm=tm),
        grid=(b, n // tm),
        in_specs=[row, row, row, row,
                  pl.BlockSpec((1, 2, 6, d), lambda bi, i: (bi, 0, 0, 0)),
                  pl.BlockSpec(w_out.shape, lambda bi, i: (0, 0))],
        out_specs=row,
        out_shape=jax.ShapeDtypeStruct((b, n, d), F32),
        compiler_params=_cparams(("parallel", "parallel")),
        name="out_proj_lru",
    )(xc, yf, yr, u, mods, w_out)


def _conv_kernel(main_ref, prev_ref, next_ref, w_ref, b_ref, o_ref, *, n_blocks, halo):
    i = pl.program_id(1)
    tb = main_ref.shape[1]
    first = jnp.logical_or(i == 0, i == 1)
    last = jnp.logical_or(i == 0, i == n_blocks - 1)
    main = main_ref[0].astype(F32)
    prev = jnp.where(first, 0.0, prev_ref[0].astype(F32))
    nxt = jnp.where(last, 0.0, next_ref[0].astype(F32))
    ext = jnp.concatenate([prev, main, nxt], axis=0)
    w = w_ref[...]
    acc = jnp.broadcast_to(b_ref[...], main.shape)
    for k in range(w.shape[0]):
        off = halo + k - 2
        acc = acc + w[k:k + 1, :] * ext[off:off + tb, :]
    o_ref[0] = acc


def _conv(u, conv_w, conv_b, n_ctx, col):
    b, n, d2 = u.shape
    d = d2 // 2
    tb = n_ctx
    halo = V7X_BF16_SUBLANES
    nb = n // tb
    hb = tb // halo
    return pl.pallas_call(
        functools.partial(_conv_kernel, n_blocks=nb, halo=halo),
        grid=(b, nb),
        in_specs=[pl.BlockSpec((1, tb, d), lambda bi, i: (bi, i, col)),
                  pl.BlockSpec((1, halo, d), lambda bi, i: (bi, jnp.maximum(i * hb - 1, 0), col)),
                  pl.BlockSpec((1, halo, d), lambda bi, i: (bi, jnp.minimum((i + 1) * hb, n // halo - 1), col)),
                  pl.BlockSpec(conv_w.shape, lambda bi, i: (0, 0)),
                  pl.BlockSpec((1, d), lambda bi, i: (0, 0))],
        out_specs=pl.BlockSpec((1, tb, d), lambda bi, i: (bi, i, 0)),
        out_shape=jax.ShapeDtypeStruct((b, n, d), F32),
        compiler_params=_cparams(("parallel", "parallel")),
        name="lru_conv",
    )(u, u, u, conv_w, conv_b.reshape(1, d))


def _lru_scan_kernel(xf_ref, xr_ref, gaw_ref, gab_ref, gxw_ref, gxb_ref, lam_ref, yf_ref, yr_ref,
                     h_scr, a_scr, b_scr):
    t = pl.program_id(1)
    tb, d = xf_ref.shape[1:]
    nblk = gaw_ref.shape[1]
    bs = d // nblk

    @pl.when(t == 0)
    def _():
        h_scr[...] = jnp.zeros(h_scr.shape, F32)

    def block_diag(xb, w_ref, dr):
        return jnp.concatenate(
            [jnp.dot(xb[:, n * bs:(n + 1) * bs], w_ref[dr, n], preferred_element_type=F32)
             for n in range(nblk)], axis=1)

    for dr, x_ref in enumerate((xf_ref, xr_ref)):
        xc = x_ref[0]
        xb = xc.astype(BF16)
        r = jax.nn.sigmoid(block_diag(xb, gaw_ref, dr) + gab_ref[dr:dr + 1, :])
        gi = jax.nn.sigmoid(block_diag(xb, gxw_ref, dr) + gxb_ref[dr:dr + 1, :])
        z = -lam_ref[dr:dr + 1, :]
        softplus = jnp.maximum(z, 0.0) + jnp.log1p(jnp.exp(-jnp.abs(z)))
        a = jnp.exp((-LRU_C) * r * softplus)
        a_scr[dr] = a
        b_scr[dr] = jnp.sqrt(1.0 - a * a) * gi * xc

    row = lax.broadcasted_iota(jnp.int32, (8, d), 0)
    nchunks = tb // 8

    def chunk(c, carry):
        r0 = pl.multiple_of(c * 8, 8)
        a = a_scr[0, pl.ds(r0, 8), :]
        bv = b_scr[0, pl.ds(r0, 8), :]
        for sh in (1, 2, 4):
            ok = row >= sh
            bv = jnp.where(ok, a * pltpu.roll(bv, sh, axis=0) + bv, bv)
            a = jnp.where(ok, a * pltpu.roll(a, sh, axis=0), a)
        hf = bv + a * h_scr[0]
        yf_ref[0, pl.ds(r0, 8), :] = hf
        h_scr[0] = hf[7:8, :]
        r1 = pl.multiple_of((nchunks - 1 - c) * 8, 8)
        a = a_scr[1, pl.ds(r1, 8), :]
        bv = b_scr[1, pl.ds(r1, 8), :]
        for sh in (1, 2, 4):
            ok = row < 8 - sh
            bv = jnp.where(ok, a * pltpu.roll(bv, 8 - sh, axis=0) + bv, bv)
            a = jnp.where(ok, a * pltpu.roll(a, 8 - sh, axis=0), a)
        hr = bv + a * h_scr[1]
        yr_ref[0, pl.ds(r1, 8), :] = hr
        h_scr[1] = hr[0:1, :]
        return carry

    lax.fori_loop(0, nchunks, chunk, 0)


def _lru_scan(xcv, ga_w, ga_b, gx_w, gx_b, lam, n_ctx):
    b, n, d = xcv.shape
    tb = n_ctx
    nt = n // tb
    fwd = pl.BlockSpec((1, tb, d), lambda bi, t: (bi, t, 0))
    rev = pl.BlockSpec((1, tb, d), lambda bi, t: (bi, jnp.where(t == 0, 0, nt - t), 0))
    wspec = pl.BlockSpec(ga_w.shape, lambda bi, t: (0, 0, 0, 0))
    vspec = pl.BlockSpec((2, d), lambda bi, t: (0, 0))
    out = jax.ShapeDtypeStruct((b, n, d), F32)
    return pl.pallas_call(
        _lru_scan_kernel,
        grid=(b, nt),
        in_specs=[fwd, rev, wspec, vspec, wspec, vspec, vspec],
        out_specs=[fwd, rev],
        out_shape=[out, out],
        scratch_shapes=[pltpu.VMEM((2, 1, d), F32), pltpu.VMEM((2, tb, d), F32),
                        pltpu.VMEM((2, tb, d), F32)],
        compiler_params=_cparams(("parallel", "arbitrary")),
        name="lru_scan",
    )(xcv, xcv, ga_w, ga_b, gx_w, gx_b, lam)


def _ffn_kernel(x_ref, mod_ref, g_ref, wg_ref, wu_ref, wd_ref, o_ref, h_scr, *, n_ctx, tm):
    k = pl.program_id(2)
    row0 = pl.program_id(1) * tm

    @pl.when(k == 0)
    def _():
        shift, scale = _row_mods(mod_ref, row0, tm, n_ctx, (3, 4))
        h_scr[...] = _norm_mod(x_ref[0], g_ref[...], shift, scale).astype(BF16)
        o_ref[0] = jnp.zeros(o_ref.shape[1:], F32)

    h = h_scr[...]
    act = _silu(jnp.dot(h, wg_ref[...], preferred_element_type=F32)) * jnp.dot(
        h, wu_ref[...], preferred_element_type=F32)
    o_ref[0] += jnp.dot(act.astype(BF16), wd_ref[...], preferred_element_type=F32)

    @pl.when(k == pl.num_programs(2) - 1)
    def _():
        (gate,) = _row_mods(mod_ref, row0, tm, n_ctx, (5,))
        o_ref[0] = x_ref[0] + gate * o_ref[0]


def _ffn(xc, mods, g, wg, wu, wd, n_ctx):
    b, n, d = xc.shape
    dff = wg.shape[1]
    tm = _pick_tile(n, 640, 128)
    tf = _pick_tile(dff, 512, V7X_LANES)
    return pl.pallas_call(
        functools.partial(_ffn_kernel, n_ctx=n_ctx, tm=tm),
        grid=(b, n // tm, dff // tf),
        in_specs=[pl.BlockSpec((1, tm, d), lambda bi, i, k: (bi, i, 0)),
                  pl.BlockSpec((1, 2, 6, d), lambda bi, i, k: (bi, 0, 0, 0)),
                  pl.BlockSpec((1, d), lambda bi, i, k: (0, 0)),
                  pl.BlockSpec((d, tf), lambda bi, i, k: (0, k)),
                  pl.BlockSpec((d, tf), lambda bi, i, k: (0, k)),
                  pl.BlockSpec((tf, d), lambda bi, i, k: (k, 0))],
        out_specs=pl.BlockSpec((1, tm, d), lambda bi, i, k: (bi, i, 0)),
        out_shape=jax.ShapeDtypeStruct((b, n, d), F32),
        scratch_shapes=[pltpu.VMEM((tm, d), BF16)],
        compiler_params=_cparams(("parallel", "parallel", "arbitrary")),
        name="ffn",
    )(xc, mods, g.reshape(1, d), wg, wu, wd)


def _qk_layout(heads):
    perm = np.zeros(heads * V_DIM, np.int32)
    gain_idx = np.zeros(V_DIM, np.int32)
    for h in range(heads):
        for half in range(2):
            for j in range(HEAD_DIM // 2):
                for mp in range(2):
                    new = h * V_DIM + half * HEAD_DIM + 2 * j + mp
                    perm[new] = h * V_DIM + mp * HEAD_DIM + 2 * j + half
                    gain_idx[half * HEAD_DIM + 2 * j + mp] = 2 * j + half
    return perm, gain_idx


def _rope_tables(n_ctx, s):
    n_rows = s // GRID_W
    rows = jnp.repeat(jnp.arange(n_rows, dtype=F32), GRID_W, total_repeat_length=s)
    cols = jnp.tile(jnp.arange(GRID_W, dtype=F32), n_rows)
    n_axis = HEAD_DIM // 4
    inv = ROPE_THETA ** (-jnp.arange(n_axis, dtype=F32) / n_axis)
    ang = jnp.concatenate([rows[:, None] * inv, cols[:, None] * inv], axis=-1)
    cos = jnp.concatenate([jnp.ones((n_ctx, HEAD_DIM // 2), F32), jnp.cos(ang)], axis=0)
    sin = jnp.concatenate([jnp.zeros((n_ctx, HEAD_DIM // 2), F32), jnp.sin(ang)], axis=0)
    cos_h = jnp.repeat(cos, 2, axis=1)
    sin_h = jnp.repeat(sin, 2, axis=1)
    return jnp.concatenate([cos_h, cos_h], axis=1), jnp.concatenate([-sin_h, sin_h], axis=1)


def kernel(x, c, ctx, c_ctx, ada_w, ada_b, norm_mix_g, norm_ffn_g, ffn_w_gate, ffn_w_up, ffn_w_down, ab_w_in, ab_w_out, ab_q_norm_g, ab_k_norm_g, ab_lam_q1, ab_lam_k1, ab_lam_q2, ab_lam_k2, ab_head_norm_g, lru_w_in, lru_w_out, lru_conv_w, lru_conv_b, lru_gate_a_w, lru_gate_a_b, lru_gate_x_w, lru_gate_x_b, lru_lambda):
    b, s, d = x.shape
    n_ctx = ctx.shape[1]
    depth = ada_w.shape[0]
    c_f = d // 2
    heads = (d // 2) // V_DIM
    dq = heads * V_DIM

    cond8 = jnp.zeros((8, d), F32).at[:b].set(c).at[b].set(c_ctx)
    ada = _adaln(cond8, ada_w, ada_b).reshape(depth, 8, 6, d)
    mods_all = jnp.stack([jnp.broadcast_to(ada[:, b][:, None], (depth, b, 6, d)), ada[:, :b]], axis=2)

    perm, gain_idx = _qk_layout(heads)
    cos_t, sin_t = _rope_tables(n_ctx, s)

    xc = jnp.concatenate([ctx, x], axis=1)

    for l in range(depth):
        j = l // 2
        mods = mods_all[l]
        if l % 2 == 0:
            lam_init = 0.8 - 0.6 * math.exp(-0.3 * l)
            w_in = ab_w_in[j]
            w_in = jnp.concatenate([w_in[:, :c_f],
                                    jnp.take(w_in[:, c_f:c_f + dq], perm, axis=1),
                                    jnp.take(w_in[:, c_f + dq:c_f + 2 * dq], perm, axis=1),
                                    w_in[:, c_f + 2 * dq:]], axis=1).astype(BF16)
            u = _proj(xc, mods, norm_mix_g[l], w_in, n_ctx)
            gq = jnp.take(ab_q_norm_g[j], gain_idx).reshape(1, V_DIM)
            gk = jnp.take(ab_k_norm_g[j], gain_idx).reshape(1, V_DIM)
            q, k = _qk_prep(u, cos_t, sin_t, gq, gk, heads, c_f // dq, c_f // dq + 1)
            v = u[:, :, c_f + 2 * dq:]
            vt = jnp.transpose(v.reshape(b, (n_ctx + s) // n_ctx, n_ctx, heads, V_DIM), (0, 3, 1, 4, 2))
            lamv = jnp.stack([ab_lam_q1[j], ab_lam_k1[j], ab_lam_q2[j], ab_lam_k2[j]])
            o = _attention(lamv, ab_head_norm_g[j].reshape(1, V_DIM), q, k, vt, n_ctx, lam_init)
            f_ctx = _fourier_ctx(u[:, :n_ctx, :c_f], N_FOURIER_GROUPS)
            f_lat = _fourier_latent(u[:, n_ctx:, :c_f], N_FOURIER_GROUPS)
            f = jnp.concatenate([f_ctx, f_lat], axis=1)
            xc = _out_ab(xc, f, o, mods, ab_w_out[j].astype(BF16), n_ctx)
        else:
            u = _proj(xc, mods, norm_mix_g[l], lru_w_in[j].astype(BF16), n_ctx)
            xcv = _conv(u, lru_conv_w[j], lru_conv_b[j], n_ctx, 1)
            yf, yr = _lru_scan(xcv, lru_gate_a_w[j].astype(BF16), lru_gate_a_b[j],
                               lru_gate_x_w[j].astype(BF16), lru_gate_x_b[j], lru_lambda[j], n_ctx)
            xc = _out_lru(xc, yf, yr, u, mods, lru_w_out[j].astype(BF16), n_ctx)
        xc = _ffn(xc, mods, norm_ffn_g[l], ffn_w_gate[l].astype(BF16), ffn_w_up[l].astype(BF16),
                  ffn_w_down[l].astype(BF16), n_ctx)
    return xc[:, n_ctx:]
```

```python
import functools
import math

import jax
import jax.numpy as jnp
import numpy as np
from jax import lax
from jax.experimental import pallas as pl
from jax.experimental.pallas import tpu as pltpu

F32 = jnp.float32
BF16 = jnp.bfloat16

EPS = 1e-6
GRID_W = 64
ROPE_THETA = 10000.0
HEAD_DIM = 64
V_DIM = 2 * HEAD_DIM
N_FOURIER_GROUPS = 4
LRU_C = 8.0
LOG2E = 1.4426950408889634

V7X_LANES = 128
V7X_BF16_SUBLANES = 16
V7X_VMEM_BYTES = 64 * 1024 * 1024
VMEM_LIMIT = V7X_VMEM_BYTES - 8 * 1024 * 1024

FOURIER_SLOW = 32
NEG_BIG = -1e30


def _cparams(sem, flags=None):
    return pltpu.CompilerParams(dimension_semantics=sem, vmem_limit_bytes=VMEM_LIMIT, flags=flags)


def _pick_tile(n, target, mult):
    best = None
    for t in range(mult, min(n, target) + 1, mult):
        if n % t == 0:
            best = t
    assert best is not None, (n, target, mult)
    return best


def _row_mods(mod_ref, row0, rows, n_ctx, idx):
    r = row0 + lax.broadcasted_iota(jnp.int32, (rows, 1), 0)
    is_ctx = r < n_ctx
    return [jnp.where(is_ctx, mod_ref[0, 0, k:k + 1, :], mod_ref[0, 1, k:k + 1, :]) for k in idx]


def _norm_mod(x, g, shift, scale):
    y = x * lax.rsqrt(jnp.mean(x * x, axis=-1, keepdims=True) + EPS)
    return (y * g) * (1.0 + scale) + shift


def _silu(x):
    return x * jax.nn.sigmoid(x)


def _gelu_tanh(x):
    return 0.5 * x * (1.0 + jnp.tanh(math.sqrt(2.0 / math.pi) * (x + 0.044715 * (x * x * x))))


def _adaln_kernel(cond_ref, w_ref, b_ref, o_ref):
    s = _silu(cond_ref[...])
    o_ref[0] = jnp.dot(s.astype(BF16), w_ref[0].astype(BF16),
                       preferred_element_type=F32) + b_ref[0]


def _adaln(cond8, ada_w, ada_b):
    depth, d, n6 = ada_w.shape
    tn = _pick_tile(n6, 1024, V7X_LANES)
    return pl.pallas_call(
        _adaln_kernel,
        grid=(depth, n6 // tn),
        in_specs=[pl.BlockSpec((8, d), lambda l, j: (0, 0)),
                  pl.BlockSpec((1, d, tn), lambda l, j: (l, 0, j)),
                  pl.BlockSpec((1, 1, tn), lambda l, j: (l, 0, j))],
        out_specs=pl.BlockSpec((1, 8, tn), lambda l, j: (l, 0, j)),
        out_shape=jax.ShapeDtypeStruct((depth, 8, n6), F32),
        compiler_params=_cparams(("parallel", "parallel")),
        name="adaln",
    )(cond8, ada_w, ada_b.reshape(depth, 1, n6))


def _proj_kernel(x_ref, mod_ref, g_ref, w_ref, o_ref, h_scr, *, n_ctx, tm):
    row0 = pl.program_id(1) * tm

    @pl.when(pl.program_id(2) == 0)
    def _():
        shift, scale = _row_mods(mod_ref, row0, tm, n_ctx, (0, 1))
        h_scr[...] = _norm_mod(x_ref[0], g_ref[...], shift, scale).astype(BF16)

    o_ref[0] = jnp.dot(h_scr[...], w_ref[...], preferred_element_type=F32).astype(o_ref.dtype)


def _proj(xc, mods, g, w, n_ctx):
    b, n, d = xc.shape
    nout = w.shape[1]
    tm = _pick_tile(n, 640, 128)
    tn = _pick_tile(nout, 1024, V7X_LANES)
    return pl.pallas_call(
        functools.partial(_proj_kernel, n_ctx=n_ctx, tm=tm),
        grid=(b, n // tm, nout // tn),
        in_specs=[pl.BlockSpec((1, tm, d), lambda bi, i, j: (bi, i, 0)),
                  pl.BlockSpec((1, 2, 6, d), lambda bi, i, j: (bi, 0, 0, 0)),
                  pl.BlockSpec((1, d), lambda bi, i, j: (0, 0)),
                  pl.BlockSpec((d, tn), lambda bi, i, j: (0, j))],
        out_specs=pl.BlockSpec((1, tm, tn), lambda bi, i, j: (bi, i, j)),
        out_shape=jax.ShapeDtypeStruct((b, n, nout), BF16),
        scratch_shapes=[pltpu.VMEM((tm, d), BF16)],
        compiler_params=_cparams(("parallel", "parallel", "arbitrary")),
        name="proj",
    )(xc, mods, g.reshape(1, d), w)


def _qk_prep_kernel(q_ref, k_ref, cos_ref, sin_ref, gq_ref, gk_ref, qo_ref, ko_ref, *, heads, q_scale):
    cos = cos_ref[...]
    sin = sin_ref[...]
    for src, g_ref, dst, sc in ((q_ref, gq_ref, qo_ref, q_scale), (k_ref, gk_ref, ko_ref, 1.0)):
        g = g_ref[...]
        for h in range(heads):
            x = src[0, :, h * V_DIM:(h + 1) * V_DIM].astype(F32)
            t = x * x
            for sh in (64, 32, 16, 8, 4, 2):
                t = t + pltpu.roll(t, sh, axis=1)
            y = x * lax.rsqrt(t * (1.0 / HEAD_DIM) + EPS) * g
            y = y * cos + pltpu.roll(y, 64, axis=1) * sin
            dst[0, :, h * V_DIM:(h + 1) * V_DIM] = (y * sc).astype(BF16)


def _qk_prep(u, cos_t, sin_t, gq, gk, heads, q_col, k_col):
    b, n, _ = u.shape
    dq = heads * V_DIM
    tm = _pick_tile(n, 640, 128)
    q_scale = (HEAD_DIM ** -0.5) * LOG2E
    out = jax.ShapeDtypeStruct((b, n, dq), BF16)
    return pl.pallas_call(
        functools.partial(_qk_prep_kernel, heads=heads, q_scale=q_scale),
        grid=(b, n // tm),
        in_specs=[pl.BlockSpec((1, tm, dq), lambda bi, i: (bi, i, q_col)),
                  pl.BlockSpec((1, tm, dq), lambda bi, i: (bi, i, k_col)),
                  pl.BlockSpec((tm, V_DIM), lambda bi, i: (i, 0)),
                  pl.BlockSpec((tm, V_DIM), lambda bi, i: (i, 0)),
                  pl.BlockSpec((1, V_DIM), lambda bi, i: (0, 0)),
                  pl.BlockSpec((1, V_DIM), lambda bi, i: (0, 0))],
        out_specs=[pl.BlockSpec((1, tm, dq), lambda bi, i: (bi, i, 0)),
                   pl.BlockSpec((1, tm, dq), lambda bi, i: (bi, i, 0))],
        out_shape=[out, out],
        compiler_params=_cparams(("parallel", "parallel")),
        name="qk_prep",
    )(u, u, cos_t, sin_t, gq, gk)


def _attn_kernel(lamv_ref, hg_ref, q_ref, k_ref, vt_ref, o_ref, m_scr, l_scr, acc_scr, cm_scr, alpha_scr,
                 s_scr, p_scr,
                 *, n_ctx, tk, n_lat_chunks, lam_init, rb):
    q = q_ref[0]
    tq = q.shape[0]
    lane = lax.broadcasted_iota(jnp.int32, q.shape, 1)
    even = (lane % 2) == 0
    zero = jnp.zeros_like(q)
    qm = (jnp.where(even, q, zero), jnp.where(even, zero, q))

    m_scr[...] = jnp.full(m_scr.shape, NEG_BIG, F32)
    l_scr[...] = jnp.zeros(l_scr.shape, F32)
    acc_scr[...] = jnp.zeros(acc_scr.shape, F32)

    def scores(kc, sb):
        rows = kc.shape[0]
        for mp in range(2):
            s = lax.dot_general(kc, qm[mp], (((1,), (1,)), ((), ())), preferred_element_type=F32)
            s_scr[sb, mp, 0:rows, :] = s
            cm_scr[sb, mp] = jnp.broadcast_to(jnp.max(s, axis=0, keepdims=True), (8, tq))

    def softmax(sb, rows):
        for mp in range(2):
            m_old = m_scr[mp]
            m_new = jnp.maximum(m_old, cm_scr[sb, mp])
            m_scr[mp] = m_new
            alpha = jnp.exp2(m_old - m_new)
            alpha_scr[sb, mp] = alpha
            lsum = jnp.zeros((8, tq), F32)
            for r0 in range(0, rows, rb):
                p = jnp.exp2(s_scr[sb, mp, r0:r0 + rb, :] - m_new[0:1, :])
                lsum = lsum + jnp.sum(p.reshape(rb // 8, 8, tq), axis=0)
                p_scr[sb, mp, r0:r0 + rb, :] = p.astype(BF16)
            l_scr[mp] = alpha * l_scr[mp] + jnp.broadcast_to(jnp.sum(lsum, axis=0, keepdims=True), (8, tq))

    def values(vtc, sb):
        rows = vtc.shape[1]
        for mp in range(2):
            acc_scr[mp] = alpha_scr[sb, mp, 0:1, :] * acc_scr[mp] + jnp.dot(
                vtc, p_scr[sb, mp, 0:rows, :], preferred_element_type=F32)

    units = tk // n_ctx

    def k_chunk(j):
        return k_ref[0, pl.ds(pl.multiple_of(n_ctx + (j - 1) * tk, n_ctx), tk), :]

    def vt_chunk(j):
        return jnp.concatenate([vt_ref[0, 0, 1 + (j - 1) * units + u] for u in range(units)], axis=1)

    n_chunks = n_lat_chunks + 1
    scores(k_ref[0, 0:n_ctx, :], 0)
    if n_chunks == 1:
        softmax(0, n_ctx)
        values(vt_ref[0, 0, 0], 0)
    else:
        assert n_chunks % 2 == 1
        scores(k_chunk(1), 1)
        softmax(0, n_ctx)
        scores(k_chunk(2), 0)
        softmax(1, tk)
        values(vt_ref[0, 0, 0], 0)

        def two_steps(i, carry):
            t = 2 * i + 1
            softmax(0, tk)
            scores(k_chunk(t + 2), 1)
            values(vt_chunk(t), 1)
            softmax(1, tk)
            scores(k_chunk(t + 3), 0)
            values(vt_chunk(t + 1), 0)
            return carry

        lax.fori_loop(0, (n_chunks - 3) // 2, two_steps, 0)
        softmax(0, tk)
        values(vt_chunk(n_chunks - 2), 1)
        values(vt_chunk(n_chunks - 1), 0)

    lv = lamv_ref[...]
    lam = (jnp.exp(jnp.sum(lv[0:1] * lv[1:2], axis=1, keepdims=True))
           - jnp.exp(jnp.sum(lv[2:3] * lv[3:4], axis=1, keepdims=True)) + lam_init)
    ot = acc_scr[0] / l_scr[0, 0:1, :] - lam * (acc_scr[1] / l_scr[1, 0:1, :])
    o = ot.T
    o = o * lax.rsqrt(jnp.mean(o * o, axis=-1, keepdims=True) + EPS)
    o_ref[0] = (o * hg_ref[...] * (1.0 - lam_init)).astype(o_ref.dtype)


def _attn_ctx_kernel(lamv_ref, hg_ref, q_ref, k_ref, vt_ref, o_prev_ref, o_ref, *scratch, **kw):
    del o_prev_ref
    _attn_kernel(lamv_ref, hg_ref, q_ref, k_ref, vt_ref, o_ref, *scratch, **kw)


def _attention(lamv, head_g, q, k, vt, n_ctx, lam_init):
    b, n, dq = q.shape
    heads = dq // V_DIM
    tq = n_ctx
    s = n - n_ctx
    tk = _pick_tile(s // 2, 1024, n_ctx)
    rb = _pick_tile(n_ctx, 128, 8)
    common = [pl.BlockSpec((4, HEAD_DIM), lambda bi, h, i: (0, 0)),
              pl.BlockSpec((1, V_DIM), lambda bi, h, i: (0, 0))]
    scratch = [pltpu.VMEM((2, 8, tq), F32), pltpu.VMEM((2, 8, tq), F32),
               pltpu.VMEM((2, V_DIM, tq), F32), pltpu.VMEM((2, 2, 8, tq), F32),
               pltpu.VMEM((2, 2, 8, tq), F32), pltpu.VMEM((2, 2, tk, tq), F32),
               pltpu.VMEM((2, 2, tk, tq), BF16)]
    kw = dict(n_ctx=n_ctx, tk=tk, lam_init=lam_init, rb=rb)
    o = pl.pallas_call(
        functools.partial(_attn_kernel, n_lat_chunks=s // tk, **kw),
        grid=(b, heads, s // tq),
        in_specs=common + [
            pl.BlockSpec((1, tq, V_DIM), lambda bi, h, i: (bi, i + 1, h)),
            pl.BlockSpec((1, n, V_DIM), lambda bi, h, i: (bi, 0, h)),
            pl.BlockSpec((1, 1, n // n_ctx, V_DIM, n_ctx), lambda bi, h, i: (bi, h, 0, 0, 0))],
        out_specs=pl.BlockSpec((1, tq, V_DIM), lambda bi, h, i: (bi, i + 1, h)),
        out_shape=jax.ShapeDtypeStruct((b, n, dq), BF16),
        scratch_shapes=scratch,
        compiler_params=_cparams(("parallel", "parallel", "arbitrary")),
        name="diff_attn",
    )(lamv, head_g, q, k, vt)
    return pl.pallas_call(
        functools.partial(_attn_ctx_kernel, n_lat_chunks=0, **kw),
        grid=(b, heads, 1),
        in_specs=common + [
            pl.BlockSpec((1, tq, V_DIM), lambda bi, h, i: (bi, 0, h)),
            pl.BlockSpec((1, n_ctx, V_DIM), lambda bi, h, i: (bi, 0, h)),
            pl.BlockSpec((1, 1, 1, V_DIM, n_ctx), lambda bi, h, i: (bi, h, 0, 0, 0)),
            pl.BlockSpec(memory_space=pl.ANY)],
        out_specs=pl.BlockSpec((1, tq, V_DIM), lambda bi, h, i: (bi, 0, h)),
        out_shape=jax.ShapeDtypeStruct((b, n, dq), BF16),
        scratch_shapes=scratch,
        input_output_aliases={5: 0},
        compiler_params=_cparams(("parallel", "parallel", "arbitrary")),
        name="diff_attn_ctx",
    )(lamv, head_g, q, k, vt, o)


def _dft_cos_sin(n, rows=None, cols=None, mod=None):
    mod = n if mod is None else mod
    r = jnp.arange(n if rows is None else rows, dtype=jnp.int32)[:, None]
    c = jnp.arange(n if cols is None else cols, dtype=jnp.int32)[None, :]
    ang = ((r * c) % mod).astype(F32) * (2.0 * math.pi / mod)
    return jnp.cos(ang), jnp.sin(ang)


def _channel_dft(x, cs_ref, groups):
    cg = x.shape[1] // groups
    wr, wi = [], []
    for g in range(groups):
        w = jnp.dot(x[:, g * cg:(g + 1) * cg], cs_ref[...], preferred_element_type=F32)
        wr.append(w[:, :cg])
        wi.append(w[:, cg:])
    return jnp.concatenate(wr, axis=1), jnp.concatenate(wi, axis=1)


def _fourier1_kernel(x_ref, cs_ref, kc_ref, ks_ref, tc_ref, ts_ref, zr_ref, zi_ref, *, groups):
    ls, tf, c = x_ref.shape[1:]
    x = x_ref[0].reshape(ls * tf, c)
    wr, wi = _channel_dft(x, cs_ref, groups)
    wr = wr.astype(BF16)
    wi = wi.astype(BF16)
    kc = kc_ref[...]
    ks = ks_ref[...]
    zr = jnp.dot(kc, wr, preferred_element_type=F32) + jnp.dot(ks, wi, preferred_element_type=F32)
    zi = jnp.dot(kc, wi, preferred_element_type=F32) - jnp.dot(ks, wr, preferred_element_type=F32)
    reps = c // V7X_LANES
    tc = jnp.concatenate([tc_ref[0]] * reps, axis=1)
    ts = jnp.concatenate([ts_ref[0]] * reps, axis=1)
    zr_ref[0] = (zr * tc + zi * ts).astype(BF16).reshape(ls, tf, c)
    zi_ref[0] = (zi * tc - zr * ts).astype(BF16).reshape(ls, tf, c)


def _fourier2_kernel(zr_ref, zi_ref, fc_ref, fs_ref, y_ref, *, scale):
    for u in range(zr_ref.shape[1]):
        y = (jnp.dot(fc_ref[...], zr_ref[0, u], preferred_element_type=F32)
             + jnp.dot(fs_ref[...], zi_ref[0, u], preferred_element_type=F32))
        y_ref[0, u] = (y * scale).astype(BF16)


def _fourier_ctx_kernel(x_ref, cs_ref, pc_ref, ps_ref, y_ref, *, groups, scale):
    wr, wi = _channel_dft(x_ref[0], cs_ref, groups)
    y = (jnp.dot(pc_ref[...], wr.astype(BF16), preferred_element_type=F32)
         + jnp.dot(ps_ref[...], wi.astype(BF16), preferred_element_type=F32))
    y_ref[0] = (y * scale).astype(BF16)


def _fourier_tables(s, c):
    cg = c // N_FOURIER_GROUPS
    cc, sc = _dft_cos_sin(cg)
    cs = jnp.concatenate([cc, -sc], axis=1).astype(BF16)
    return cg, cs


def _fourier_latent(fx, groups):
    b, s, c = fx.shape
    cg, cs = _fourier_tables(s, c)
    ls = FOURIER_SLOW
    lf = s // ls
    tf = V7X_BF16_SUBLANES
    assert ls * lf == s and lf % tf == 0
    rows = ls * tf
    pc, ps = _dft_cos_sin(ls)
    eye = jnp.eye(tf, dtype=F32)
    kc = jnp.kron(pc, eye).astype(BF16)
    ks = jnp.kron(ps, eye).astype(BF16)
    u_idx = jnp.repeat(jnp.arange(ls, dtype=jnp.int32), tf)[None, :]
    f_idx = (jnp.arange(lf // tf, dtype=jnp.int32)[:, None] * tf
             + jnp.tile(jnp.arange(tf, dtype=jnp.int32), ls)[None, :])
    ang = ((u_idx * f_idx) % s).astype(F32) * (2.0 * math.pi / s)
    tc = jnp.broadcast_to(jnp.cos(ang)[:, :, None], (lf // tf, rows, V7X_LANES))
    ts = jnp.broadcast_to(jnp.sin(ang)[:, :, None], (lf // tf, rows, V7X_LANES))

    x4 = fx.reshape(b, ls, lf, c)
    zshape = jax.ShapeDtypeStruct((b, ls, lf, c), BF16)
    blk = pl.BlockSpec((1, ls, tf, c), lambda bi, i: (bi, 0, i, 0))
    zr, zi = pl.pallas_call(
        functools.partial(_fourier1_kernel, groups=groups),
        grid=(b, lf // tf),
        in_specs=[blk,
                  pl.BlockSpec((cg, 2 * cg), lambda bi, i: (0, 0)),
                  pl.BlockSpec((rows, rows), lambda bi, i: (0, 0)),
                  pl.BlockSpec((rows, rows), lambda bi, i: (0, 0)),
                  pl.BlockSpec((1, rows, V7X_LANES), lambda bi, i: (i, 0, 0)),
                  pl.BlockSpec((1, rows, V7X_LANES), lambda bi, i: (i, 0, 0))],
        out_specs=[blk, blk],
        out_shape=[zshape, zshape],
        compiler_params=_cparams(("parallel", "parallel")),
        name="fourier_slow",
    )(x4, cs, kc, ks, tc, ts)

    fc, fs = _dft_cos_sin(lf)
    tu = _pick_tile(ls, 4, 1)
    blk2 = pl.BlockSpec((1, tu, lf, c), lambda bi, i: (bi, i, 0, 0))
    yt = pl.pallas_call(
        functools.partial(_fourier2_kernel, scale=1.0 / math.sqrt(s * cg)),
        grid=(b, ls // tu),
        in_specs=[blk2, blk2,
                  pl.BlockSpec((lf, lf), lambda bi, i: (0, 0)),
                  pl.BlockSpec((lf, lf), lambda bi, i: (0, 0))],
        out_specs=blk2,
        out_shape=zshape,
        compiler_params=_cparams(("parallel", "parallel")),
        name="fourier_fast",
    )(zr, zi, fc.astype(BF16), fs.astype(BF16))
    return jnp.swapaxes(yt, 1, 2).reshape(b, s, c)


def _fourier_ctx(fxc, groups):
    b, n_ctx, c = fxc.shape
    cg, cs = _fourier_tables(n_ctx, c)
    pc, ps = _dft_cos_sin(n_ctx)
    return pl.pallas_call(
        functools.partial(_fourier_ctx_kernel, groups=groups, scale=1.0 / math.sqrt(n_ctx * cg)),
        grid=(b,),
        in_specs=[pl.BlockSpec((1, n_ctx, c), lambda bi: (bi, 0, 0)),
                  pl.BlockSpec((cg, 2 * cg), lambda bi: (0, 0)),
                  pl.BlockSpec((n_ctx, n_ctx), lambda bi: (0, 0)),
                  pl.BlockSpec((n_ctx, n_ctx), lambda bi: (0, 0))],
        out_specs=pl.BlockSpec((1, n_ctx, c), lambda bi: (bi, 0, 0)),
        out_shape=jax.ShapeDtypeStruct((b, n_ctx, c), BF16),
        compiler_params=_cparams(("parallel",)),
        name="fourier_ctx",
    )(fxc, cs, pc.astype(BF16), ps.astype(BF16))


def _out_ab_kernel(x_ref, f_ref, o_ref_in, mod_ref, w_ref, out_ref, *, n_ctx, tm, c):
    (gate,) = _row_mods(mod_ref, pl.program_id(1) * tm, tm, n_ctx, (2,))
    y = (jnp.dot(f_ref[0], w_ref[0:c, :], preferred_element_type=F32)
         + jnp.dot(o_ref_in[0], w_ref[c:, :], preferred_element_type=F32))
    out_ref[0] = x_ref[0] + gate * y


def _out_ab(xc, f, o, mods, w_out, n_ctx):
    b, n, d = xc.shape
    c = f.shape[2]
    tm = _pick_tile(n, 640, 128)
    return pl.pallas_call(
        functools.partial(_out_ab_kernel, n_ctx=n_ctx, tm=tm, c=c),
        grid=(b, n // tm),
        in_specs=[pl.BlockSpec((1, tm, d), lambda bi, i: (bi, i, 0)),
                  pl.BlockSpec((1, tm, c), lambda bi, i: (bi, i, 0)),
                  pl.BlockSpec((1, tm, o.shape[2]), lambda bi, i: (bi, i, 0)),
                  pl.BlockSpec((1, 2, 6, d), lambda bi, i: (bi, 0, 0, 0)),
                  pl.BlockSpec(w_out.shape, lambda bi, i: (0, 0))],
        out_specs=pl.BlockSpec((1, tm, d), lambda bi, i: (bi, i, 0)),
        out_shape=jax.ShapeDtypeStruct((b, n, d), F32),
        compiler_params=_cparams(("parallel", "parallel")),
        name="out_proj_attn",
    )(xc, f, o, mods, w_out)


def _out_lru_kernel(x_ref, yf_ref, yr_ref, gate_ref, mod_ref, w_ref, out_ref, *, n_ctx, tm):
    (gate,) = _row_mods(mod_ref, pl.program_id(1) * tm, tm, n_ctx, (2,))
    z = (yf_ref[0] + yr_ref[0]) * _gelu_tanh(gate_ref[0].astype(F32))
    y = jnp.dot(z.astype(BF16), w_ref[...], preferred_element_type=F32)
    out_ref[0] = x_ref[0] + gate * y


def _out_lru(xc, yf, yr, u, mods, w_out, n_ctx):
    b, n, d = xc.shape
    tm = _pick_tile(n, 320, 64)
    row = pl.BlockSpec((1, tm, d), lambda bi, i: (bi, i, 0))
    return pl.pallas_call(
        functools.partial(_out_lru_kernel, n_ctx=n_ctx, tm=tm),
        grid=(b, n // tm),
        in_specs=[row, row, row, row,
                  pl.BlockSpec((1, 2, 6, d), lambda bi, i: (bi, 0, 0, 0)),
                  pl.BlockSpec(w_out.shape, lambda bi, i: (0, 0))],
        out_specs=row,
        out_shape=jax.ShapeDtypeStruct((b, n, d), F32),
        compiler_params=_cparams(("parallel", "parallel")),
        name="out_proj_lru",
    )(xc, yf, yr, u, mods, w_out)


def _conv_kernel(main_ref, prev_ref, next_ref, w_ref, b_ref, o_ref, *, n_blocks, halo):
    i = pl.program_id(1)
    tb = main_ref.shape[1]
    first = jnp.logical_or(i == 0, i == 1)
    last = jnp.logical_or(i == 0, i == n_blocks - 1)
    main = main_ref[0].astype(F32)
    prev = jnp.where(first, 0.0, prev_ref[0].astype(F32))
    nxt = jnp.where(last, 0.0, next_ref[0].astype(F32))
    ext = jnp.concatenate([prev, main, nxt], axis=0)
    w = w_ref[...]
    acc = jnp.broadcast_to(b_ref[...], main.shape)
    for k in range(w.shape[0]):
        off = halo + k - 2
        acc = acc + w[k:k + 1, :] * ext[off:off + tb, :]
    o_ref[0] = acc


def _conv(u, conv_w, conv_b, n_ctx, col):
    b, n, d2 = u.shape
    d = d2 // 2
    tb = n_ctx
    halo = V7X_BF16_SUBLANES
    nb = n // tb
    hb = tb // halo
    return pl.pallas_call(
        functools.partial(_conv_kernel, n_blocks=nb, halo=halo),
        grid=(b, nb),
        in_specs=[pl.BlockSpec((1, tb, d), lambda bi, i: (bi, i, col)),
                  pl.BlockSpec((1, halo, d), lambda bi, i: (bi, jnp.maximum(i * hb - 1, 0), col)),
                  pl.BlockSpec((1, halo, d), lambda bi, i: (bi, jnp.minimum((i + 1) * hb, n // halo - 1), col)),
                  pl.BlockSpec(conv_w.shape, lambda bi, i: (0, 0)),
                  pl.BlockSpec((1, d), lambda bi, i: (0, 0))],
        out_specs=pl.BlockSpec((1, tb, d), lambda bi, i: (bi, i, 0)),
        out_shape=jax.ShapeDtypeStruct((b, n, d), F32),
        compiler_params=_cparams(("parallel", "parallel")),
        name="lru_conv",
    )(u, u, u, conv_w, conv_b.reshape(1, d))


def _lru_scan_kernel(xf_ref, xr_ref, gaw_ref, gab_ref, gxw_ref, gxb_ref, lam_ref, yf_ref, yr_ref,
                     h_scr, a_scr, b_scr):
    t = pl.program_id(1)
    tb, d = xf_ref.shape[1:]
    nblk = gaw_ref.shape[1]
    bs = d // nblk

    @pl.when(t == 0)
    def _():
        h_scr[...] = jnp.zeros(h_scr.shape, F32)

    def block_diag(xb, w_ref, dr):
        return jnp.concatenate(
            [jnp.dot(xb[:, n * bs:(n + 1) * bs], w_ref[dr, n], preferred_element_type=F32)
             for n in range(nblk)], axis=1)

    for dr, x_ref in enumerate((xf_ref, xr_ref)):
        xc = x_ref[0]
        xb = xc.astype(BF16)
        r = jax.nn.sigmoid(block_diag(xb, gaw_ref, dr) + gab_ref[dr:dr + 1, :])
        gi = jax.nn.sigmoid(block_diag(xb, gxw_ref, dr) + gxb_ref[dr:dr + 1, :])
        z = -lam_ref[dr:dr + 1, :]
        softplus = jnp.maximum(z, 0.0) + jnp.log1p(jnp.exp(-jnp.abs(z)))
        a = jnp.exp((-LRU_C) * r * softplus)
        a_scr[dr] = a
        b_scr[dr] = jnp.sqrt(1.0 - a * a) * gi * xc

    row = lax.broadcasted_iota(jnp.int32, (8, d), 0)
    nchunks = tb // 8

    def chunk(c, carry):
        r0 = pl.multiple_of(c * 8, 8)
        a = a_scr[0, pl.ds(r0, 8), :]
        bv = b_scr[0, pl.ds(r0, 8), :]
        for sh in (1, 2, 4):
            ok = row >= sh
            bv = jnp.where(ok, a * pltpu.roll(bv, sh, axis=0) + bv, bv)
            a = jnp.where(ok, a * pltpu.roll(a, sh, axis=0), a)
        hf = bv + a * h_scr[0]
        yf_ref[0, pl.ds(r0, 8), :] = hf
        h_scr[0] = hf[7:8, :]
        r1 = pl.multiple_of((nchunks - 1 - c) * 8, 8)
        a = a_scr[1, pl.ds(r1, 8), :]
        bv = b_scr[1, pl.ds(r1, 8), :]
        for sh in (1, 2, 4):
            ok = row < 8 - sh
            bv = jnp.where(ok, a * pltpu.roll(bv, 8 - sh, axis=0) + bv, bv)
            a = jnp.where(ok, a * pltpu.roll(a, 8 - sh, axis=0), a)
        hr = bv + a * h_scr[1]
        yr_ref[0, pl.ds(r1, 8), :] = hr
        h_scr[1] = hr[0:1, :]
        return carry

    lax.fori_loop(0, nchunks, chunk, 0)


def _lru_scan(xcv, ga_w, ga_b, gx_w, gx_b, lam, n_ctx):
    b, n, d = xcv.shape
    tb = n_ctx
    nt = n // tb
    fwd = pl.BlockSpec((1, tb, d), lambda bi, t: (bi, t, 0))
    rev = pl.BlockSpec((1, tb, d), lambda bi, t: (bi, jnp.where(t == 0, 0, nt - t), 0))
    wspec = pl.BlockSpec(ga_w.shape, lambda bi, t: (0, 0, 0, 0))
    vspec = pl.BlockSpec((2, d), lambda bi, t: (0, 0))
    out = jax.ShapeDtypeStruct((b, n, d), F32)
    return pl.pallas_call(
        _lru_scan_kernel,
        grid=(b, nt),
        in_specs=[fwd, rev, wspec, vspec, wspec, vspec, vspec],
        out_specs=[fwd, rev],
        out_shape=[out, out],
        scratch_shapes=[pltpu.VMEM((2, 1, d), F32), pltpu.VMEM((2, tb, d), F32),
                        pltpu.VMEM((2, tb, d), F32)],
        compiler_params=_cparams(("parallel", "arbitrary")),
        name="lru_scan",
    )(xcv, xcv, ga_w, ga_b, gx_w, gx_b, lam)


def _ffn_kernel(x_ref, mod_ref, g_ref, wg_ref, wu_ref, wd_ref, o_ref, h_scr, *, n_ctx, tm):
    k = pl.program_id(2)
    row0 = pl.program_id(1) * tm

    @pl.when(k == 0)
    def _():
        shift, scale = _row_mods(mod_ref, row0, tm, n_ctx, (3, 4))
        h_scr[...] = _norm_mod(x_ref[0], g_ref[...], shift, scale).astype(BF16)
        o_ref[0] = jnp.zeros(o_ref.shape[1:], F32)

    h = h_scr[...]
    act = _silu(jnp.dot(h, wg_ref[...], preferred_element_type=F32)) * jnp.dot(
        h, wu_ref[...], preferred_element_type=F32)
    o_ref[0] += jnp.dot(act.astype(BF16), wd_ref[...], preferred_element_type=F32)

    @pl.when(k == pl.num_programs(2) - 1)
    def _():
        (gate,) = _row_mods(mod_ref, row0, tm, n_ctx, (5,))
        o_ref[0] = x_ref[0] + gate * o_ref[0]


def _ffn(xc, mods, g, wg, wu, wd, n_ctx):
    b, n, d = xc.shape
    dff = wg.shape[1]
    tm = _pick_tile(n, 640, 128)
    tf = _pick_tile(dff, 512, V7X_LANES)
    return pl.pallas_call(
        functools.partial(_ffn_kernel, n_ctx=n_ctx, tm=tm),
        grid=(b, n // tm, dff // tf),
        in_specs=[pl.BlockSpec((1, tm, d), lambda bi, i, k: (bi, i, 0)),
                  pl.BlockSpec((1, 2, 6, d), lambda bi, i, k: (bi, 0, 0, 0)),
                  pl.BlockSpec((1, d), lambda bi, i, k: (0, 0)),
                  pl.BlockSpec((d, tf), lambda bi, i, k: (0, k)),
                  pl.BlockSpec((d, tf), lambda bi, i, k: (0, k)),
                  pl.BlockSpec((tf, d), lambda bi, i, k: (k, 0))],
        out_specs=pl.BlockSpec((1, tm, d), lambda bi, i, k: (bi, i, 0)),
        out_shape=jax.ShapeDtypeStruct((b, n, d), F32),
        scratch_shapes=[pltpu.VMEM((tm, d), BF16)],
        compiler_params=_cparams(("parallel", "parallel", "arbitrary")),
        name="ffn",
    )(xc, mods, g.reshape(1, d), wg, wu, wd)


def _qk_layout(heads):
    perm = np.zeros(heads * V_DIM, np.int32)
    gain_idx = np.zeros(V_DIM, np.int32)
    for h in range(heads):
        for half in range(2):
            for j in range(HEAD_DIM // 2):
                for mp in range(2):
                    new = h * V_DIM + half * HEAD_DIM + 2 * j + mp
                    perm[new] = h * V_DIM + mp * HEAD_DIM + 2 * j + half
                    gain_idx[half * HEAD_DIM + 2 * j + mp] = 2 * j + half
    return perm, gain_idx


def _rope_tables(n_ctx, s):
    n_rows = s // GRID_W
    rows = jnp.repeat(jnp.arange(n_rows, dtype=F32), GRID_W, total_repeat_length=s)
    cols = jnp.tile(jnp.arange(GRID_W, dtype=F32), n_rows)
    n_axis = HEAD_DIM // 4
    inv = ROPE_THETA ** (-jnp.arange(n_axis, dtype=F32) / n_axis)
    ang = jnp.concatenate([rows[:, None] * inv, cols[:, None] * inv], axis=-1)
    cos = jnp.concatenate([jnp.ones((n_ctx, HEAD_DIM // 2), F32), jnp.cos(ang)], axis=0)
    sin = jnp.concatenate([jnp.zeros((n_ctx, HEAD_DIM // 2), F32), jnp.sin(ang)], axis=0)
    cos_h = jnp.repeat(cos, 2, axis=1)
    sin_h = jnp.repeat(sin, 2, axis=1)
    return jnp.concatenate([cos_h, cos_h], axis=1), jnp.concatenate([-sin_h, sin_h], axis=1)


def kernel(x, c, ctx, c_ctx, ada_w, ada_b, norm_mix_g, norm_ffn_g, ffn_w_gate, ffn_w_up, ffn_w_down, ab_w_in, ab_w_out, ab_q_norm_g, ab_k_norm_g, ab_lam_q1, ab_lam_k1, ab_lam_q2, ab_lam_k2, ab_head_norm_g, lru_w_in, lru_w_out, lru_conv_w, lru_conv_b, lru_gate_a_w, lru_gate_a_b, lru_gate_x_w, lru_gate_x_b, lru_lambda):
    b, s, d = x.shape
    n_ctx = ctx.shape[1]
    depth = ada_w.shape[0]
    c_f = d // 2
    heads = (d // 2) // V_DIM
    dq = heads * V_DIM

    cond8 = jnp.zeros((8, d), F32).at[:b].set(c).at[b].set(c_ctx)
    ada = _adaln(cond8, ada_w, ada_b).reshape(depth, 8, 6, d)
    mods_all = jnp.stack([jnp.broadcast_to(ada[:, b][:, None], (depth, b, 6, d)), ada[:, :b]], axis=2)

    perm, gain_idx = _qk_layout(heads)
    cos_t, sin_t = _rope_tables(n_ctx, s)

    xc = jnp.concatenate([ctx, x], axis=1)

    for l in range(depth):
        j = l // 2
        mods = mods_all[l]
        if l % 2 == 0:
            lam_init = 0.8 - 0.6 * math.exp(-0.3 * l)
            w_in = ab_w_in[j]
            w_in = jnp.concatenate([w_in[:, :c_f],
                                    jnp.take(w_in[:, c_f:c_f + dq], perm, axis=1),
                                    jnp.take(w_in[:, c_f + dq:c_f + 2 * dq], perm, axis=1),
                                    w_in[:, c_f + 2 * dq:]], axis=1).astype(BF16)
            u = _proj(xc, mods, norm_mix_g[l], w_in, n_ctx)
            gq = jnp.take(ab_q_norm_g[j], gain_idx).reshape(1, V_DIM)
            gk = jnp.take(ab_k_norm_g[j], gain_idx).reshape(1, V_DIM)
            q, k = _qk_prep(u, cos_t, sin_t, gq, gk, heads, c_f // dq, c_f // dq + 1)
            v = u[:, :, c_f + 2 * dq:]
            vt = jnp.transpose(v.reshape(b, (n_ctx + s) // n_ctx, n_ctx, heads, V_DIM), (0, 3, 1, 4, 2))
            lamv = jnp.stack([ab_lam_q1[j], ab_lam_k1[j], ab_lam_q2[j], ab_lam_k2[j]])
            o = _attention(lamv, ab_head_norm_g[j].reshape(1, V_DIM), q, k, vt, n_ctx, lam_init)
            f_ctx = _fourier_ctx(u[:, :n_ctx, :c_f], N_FOURIER_GROUPS)
            f_lat = _fourier_latent(u[:, n_ctx:, :c_f], N_FOURIER_GROUPS)
            f = jnp.concatenate([f_ctx, f_lat], axis=1)
            xc = _out_ab(xc, f, o, mods, ab_w_out[j].astype(BF16), n_ctx)
        else:
            u = _proj(xc, mods, norm_mix_g[l], lru_w_in[j].astype(BF16), n_ctx)
            xcv = _conv(u, lru_conv_w[j], lru_conv_b[j], n_ctx, 1)
            yf, yr = _lru_scan(xcv, lru_gate_a_w[j].astype(BF16), lru_gate_a_b[j],
                               lru_gate_x_w[j].astype(BF16), lru_gate_x_b[j], lru_lambda[j], n_ctx)
            xc = _out_lru(xc, yf, yr, u, mods, lru_w_out[j].astype(BF16), n_ctx)
        xc = _ffn(xc, mods, norm_ffn_g[l], ffn_w_gate[l].astype(BF16), ffn_w_up[l].astype(BF16),
                  ffn_w_down[l].astype(BF16), n_ctx)
    return xc[:, n_ctx:]
```

```python
import functools
import math

import jax
import jax.numpy as jnp
import numpy as np
from jax import lax
from jax.experimental import pallas as pl
from jax.experimental.pallas import tpu as pltpu

F32 = jnp.float32
BF16 = jnp.bfloat16

EPS = 1e-6
GRID_W = 64
ROPE_THETA = 10000.0
HEAD_DIM = 64
V_DIM = 2 * HEAD_DIM
N_FOURIER_GROUPS = 4
LRU_C = 8.0
LOG2E = 1.4426950408889634

V7X_LANES = 128
V7X_BF16_SUBLANES = 16
V7X_VMEM_BYTES = 64 * 1024 * 1024
VMEM_LIMIT = V7X_VMEM_BYTES - 8 * 1024 * 1024

FOURIER_SLOW = 32
NEG_BIG = -1e30
BOUND_MARGIN = 1.02
SAFE_SCORE_BOUND = 50.0


def _cparams(sem, flags=None):
    return pltpu.CompilerParams(dimension_semantics=sem, vmem_limit_bytes=VMEM_LIMIT, flags=flags)


def _pick_tile(n, target, mult):
    best = None
    for t in range(mult, min(n, target) + 1, mult):
        if n % t == 0:
            best = t
    assert best is not None, (n, target, mult)
    return best


def _row_mods(mod_ref, row0, rows, n_ctx, idx):
    r = row0 + lax.broadcasted_iota(jnp.int32, (rows, 1), 0)
    is_ctx = r < n_ctx
    return [jnp.where(is_ctx, mod_ref[0, 0, k:k + 1, :], mod_ref[0, 1, k:k + 1, :]) for k in idx]


def _norm_mod_into(h_scr, x, g, mod_ref, row0, n_ctx, i_shift, i_scale):
    y = x * lax.rsqrt(jnp.mean(x * x, axis=-1, keepdims=True) + EPS)

    @pl.when(row0 >= n_ctx)
    def _():
        a = g * (1.0 + mod_ref[0, 1, i_scale:i_scale + 1, :])
        h_scr[...] = (y * a + mod_ref[0, 1, i_shift:i_shift + 1, :]).astype(BF16)

    @pl.when(row0 < n_ctx)
    def _():
        shift, scale = _row_mods(mod_ref, row0, x.shape[0], n_ctx, (i_shift, i_scale))
        h_scr[...] = ((y * g) * (1.0 + scale) + shift).astype(BF16)


def _silu(x):
    return x * jax.nn.sigmoid(x)


def _gelu_tanh(x):
    return 0.5 * x * (1.0 + jnp.tanh(math.sqrt(2.0 / math.pi) * (x + 0.044715 * (x * x * x))))


def _adaln_kernel(cond_ref, w_ref, b_ref, o_ref):
    s = _silu(cond_ref[...])
    o_ref[0] = jnp.dot(s.astype(BF16), w_ref[0].astype(BF16),
                       preferred_element_type=F32) + b_ref[0]


def _adaln(cond8, ada_w, ada_b):
    depth, d, n6 = ada_w.shape
    tn = _pick_tile(n6, 1024, V7X_LANES)
    return pl.pallas_call(
        _adaln_kernel,
        grid=(depth, n6 // tn),
        in_specs=[pl.BlockSpec((8, d), lambda l, j: (0, 0)),
                  pl.BlockSpec((1, d, tn), lambda l, j: (l, 0, j)),
                  pl.BlockSpec((1, 1, tn), lambda l, j: (l, 0, j))],
        out_specs=pl.BlockSpec((1, 8, tn), lambda l, j: (l, 0, j)),
        out_shape=jax.ShapeDtypeStruct((depth, 8, n6), F32),
        compiler_params=_cparams(("parallel", "parallel")),
        name="adaln",
    )(cond8, ada_w, ada_b.reshape(depth, 1, n6))


def _proj_kernel(x_ref, mod_ref, g_ref, w_ref, o_ref, h_scr, *, n_ctx, tm):
    row0 = pl.program_id(1) * tm

    @pl.when(pl.program_id(2) == 0)
    def _():
        _norm_mod_into(h_scr, x_ref[0], g_ref[...], mod_ref, row0, n_ctx, 0, 1)

    o_ref[0] = jnp.dot(h_scr[...], w_ref[...], preferred_element_type=F32).astype(o_ref.dtype)


def _proj(xc, mods, g, w, n_ctx):
    b, n, d = xc.shape
    nout = w.shape[1]
    tm = _pick_tile(n, 640, 128)
    tn = _pick_tile(nout, 1024, V7X_LANES)
    return pl.pallas_call(
        functools.partial(_proj_kernel, n_ctx=n_ctx, tm=tm),
        grid=(b, n // tm, nout // tn),
        in_specs=[pl.BlockSpec((1, tm, d), lambda bi, i, j: (bi, i, 0)),
                  pl.BlockSpec((1, 2, 6, d), lambda bi, i, j: (bi, 0, 0, 0)),
                  pl.BlockSpec((1, d), lambda bi, i, j: (0, 0)),
                  pl.BlockSpec((d, tn), lambda bi, i, j: (0, j))],
        out_specs=pl.BlockSpec((1, tm, tn), lambda bi, i, j: (bi, i, j)),
        out_shape=jax.ShapeDtypeStruct((b, n, nout), BF16),
        scratch_shapes=[pltpu.VMEM((tm, d), BF16)],
        compiler_params=_cparams(("parallel", "parallel", "arbitrary")),
        name="proj",
    )(xc, mods, g.reshape(1, d), w)


def _qk_prep_kernel(q_ref, k_ref, cos_ref, sin_ref, gq_ref, gk_ref, qo_ref, ko_ref, *, heads, q_scale):
    cos = cos_ref[...]
    sin = sin_ref[...]
    for src, g_ref, dst, sc in ((q_ref, gq_ref, qo_ref, q_scale), (k_ref, gk_ref, ko_ref, 1.0)):
        g = g_ref[...]
        for h in range(heads):
            x = src[0, :, h * V_DIM:(h + 1) * V_DIM].astype(F32)
            t = x * x
            for sh in (64, 32, 16, 8, 4, 2):
                t = t + pltpu.roll(t, sh, axis=1)
            y = x * lax.rsqrt(t * (1.0 / HEAD_DIM) + EPS) * g
            y = y * cos + pltpu.roll(y, 64, axis=1) * sin
            dst[0, :, h * V_DIM:(h + 1) * V_DIM] = (y * sc).astype(BF16)


def _qk_prep(u, cos_t, sin_t, gq, gk, heads, q_col, k_col):
    b, n, _ = u.shape
    dq = heads * V_DIM
    tm = _pick_tile(n, 640, 128)
    q_scale = (HEAD_DIM ** -0.5) * LOG2E
    out = jax.ShapeDtypeStruct((b, n, dq), BF16)
    return pl.pallas_call(
        functools.partial(_qk_prep_kernel, heads=heads, q_scale=q_scale),
        grid=(b, n // tm),
        in_specs=[pl.BlockSpec((1, tm, dq), lambda bi, i: (bi, i, q_col)),
                  pl.BlockSpec((1, tm, dq), lambda bi, i: (bi, i, k_col)),
                  pl.BlockSpec((tm, V_DIM), lambda bi, i: (i, 0)),
                  pl.BlockSpec((tm, V_DIM), lambda bi, i: (i, 0)),
                  pl.BlockSpec((1, V_DIM), lambda bi, i: (0, 0)),
                  pl.BlockSpec((1, V_DIM), lambda bi, i: (0, 0))],
        out_specs=[pl.BlockSpec((1, tm, dq), lambda bi, i: (bi, i, 0)),
                   pl.BlockSpec((1, tm, dq), lambda bi, i: (bi, i, 0))],
        out_shape=[out, out],
        compiler_params=_cparams(("parallel", "parallel")),
        name="qk_prep",
    )(u, u, cos_t, sin_t, gq, gk)


def _attn_kernel(lamv_ref, hg_ref, q_ref, k_ref, vt_ref, o_ref, m_scr, l_scr, acc_scr, cm_scr, alpha_scr,
                 s_scr, p_scr,
                 *, n_ctx, tk, n_lat_chunks, lam_init, rb):
    q = q_ref[0]
    tq = q.shape[0]
    lane = lax.broadcasted_iota(jnp.int32, q.shape, 1)
    even = (lane % 2) == 0
    zero = jnp.zeros_like(q)
    qm = (jnp.where(even, q, zero), jnp.where(even, zero, q))

    m_scr[...] = jnp.full(m_scr.shape, NEG_BIG, F32)
    l_scr[...] = jnp.zeros(l_scr.shape, F32)
    acc_scr[...] = jnp.zeros(acc_scr.shape, F32)

    def scores(kc, sb):
        rows = kc.shape[0]
        for mp in range(2):
            s = lax.dot_general(kc, qm[mp], (((1,), (1,)), ((), ())), preferred_element_type=F32)
            s_scr[sb, mp, 0:rows, :] = s
            cm_scr[sb, mp] = jnp.broadcast_to(jnp.max(s, axis=0, keepdims=True), (8, tq))

    def softmax(sb, rows):
        for mp in range(2):
            m_old = m_scr[mp]
            m_new = jnp.maximum(m_old, cm_scr[sb, mp])
            m_scr[mp] = m_new
            alpha = jnp.exp2(m_old - m_new)
            alpha_scr[sb, mp] = alpha
            lsum = jnp.zeros((8, tq), F32)
            for r0 in range(0, rows, rb):
                p = jnp.exp2(s_scr[sb, mp, r0:r0 + rb, :] - m_new[0:1, :])
                lsum = lsum + jnp.sum(p.reshape(rb // 8, 8, tq), axis=0)
                p_scr[sb, mp, r0:r0 + rb, :] = p.astype(BF16)
            l_scr[mp] = alpha * l_scr[mp] + jnp.broadcast_to(jnp.sum(lsum, axis=0, keepdims=True), (8, tq))

    def values(vtc, sb):
        rows = vtc.shape[1]
        for mp in range(2):
            acc_scr[mp] = alpha_scr[sb, mp, 0:1, :] * acc_scr[mp] + jnp.dot(
                vtc, p_scr[sb, mp, 0:rows, :], preferred_element_type=F32)

    units = tk // n_ctx

    def k_chunk(j):
        return k_ref[0, pl.ds(pl.multiple_of(n_ctx + (j - 1) * tk, n_ctx), tk), :]

    def vt_chunk(j):
        return jnp.concatenate([vt_ref[0, 0, 1 + (j - 1) * units + u] for u in range(units)], axis=1)

    n_chunks = n_lat_chunks + 1
    scores(k_ref[0, 0:n_ctx, :], 0)
    if n_chunks == 1:
        softmax(0, n_ctx)
        values(vt_ref[0, 0, 0], 0)
    else:
        assert n_chunks % 2 == 1
        scores(k_chunk(1), 1)
        softmax(0, n_ctx)
        scores(k_chunk(2), 0)
        softmax(1, tk)
        values(vt_ref[0, 0, 0], 0)

        def two_steps(i, carry):
            t = 2 * i + 1
            softmax(0, tk)
            scores(k_chunk(t + 2), 1)
            values(vt_chunk(t), 1)
            softmax(1, tk)
            scores(k_chunk(t + 3), 0)
            values(vt_chunk(t + 1), 0)
            return carry

        lax.fori_loop(0, (n_chunks - 3) // 2, two_steps, 0)
        softmax(0, tk)
        values(vt_chunk(n_chunks - 2), 1)
        values(vt_chunk(n_chunks - 1), 0)

    lv = lamv_ref[...]
    lam = (jnp.exp(jnp.sum(lv[0:1] * lv[1:2], axis=1, keepdims=True))
           - jnp.exp(jnp.sum(lv[2:3] * lv[3:4], axis=1, keepdims=True)) + lam_init)
    ot = acc_scr[0] / l_scr[0, 0:1, :] - lam * (acc_scr[1] / l_scr[1, 0:1, :])
    o = ot.T
    o = o * lax.rsqrt(jnp.mean(o * o, axis=-1, keepdims=True) + EPS)
    o_ref[0] = (o * hg_ref[...] * (1.0 - lam_init)).astype(o_ref.dtype)


def _attn_fast_kernel(kb_ref, lamv_ref, hg_ref, q_ref, k_ref, vt_ref, o_ref, l_scr, acc_scr, p_scr,
                      *, n_ctx, tk, n_lat_chunks, lam_init, rb):
    q = q_ref[0]
    tq = q.shape[0]
    lane = lax.broadcasted_iota(jnp.int32, q.shape, 1)
    even = (lane % 2) == 0
    zero = jnp.zeros_like(q)
    qm = (jnp.where(even, q, zero), jnp.where(even, zero, q))
    ones = jnp.ones((8, V_DIM), BF16)
    nt = (((1,), (1,)), ((), ()))
    ref = []
    for mp in range(2):
        qf = qm[mp].astype(F32)
        qn2 = lax.dot_general(ones, (qf * qf).astype(BF16), nt, preferred_element_type=F32)
        kb = jnp.concatenate([kb_ref[0, 0, mp]] * (tq // V7X_LANES), axis=1)
        ref.append(jnp.sqrt(qn2) * kb * BOUND_MARGIN)

    l_scr[...] = jnp.zeros(l_scr.shape, F32)
    acc_scr[...] = jnp.zeros(acc_scr.shape, F32)

    def scores_softmax(kc, pb):
        rows = kc.shape[0]
        for mp in range(2):
            lsum = jnp.zeros((8, tq), F32)
            for r0 in range(0, rows, rb):
                s = lax.dot_general(kc[r0:r0 + rb, :], qm[mp], nt, preferred_element_type=F32)
                p = jnp.exp2(s - ref[mp][0:1, :])
                lsum = lsum + jnp.sum(p.reshape(rb // 8, 8, tq), axis=0)
                p_scr[pb, mp, r0:r0 + rb, :] = p.astype(BF16)
            l_scr[mp] += lsum

    def values(vtc, pb):
        rows = vtc.shape[1]
        for mp in range(2):
            acc_scr[mp] += jnp.dot(vtc, p_scr[pb, mp, 0:rows, :], preferred_element_type=F32)

    units = tk // n_ctx

    def k_chunk(j):
        return k_ref[0, pl.ds(pl.multiple_of(n_ctx + (j - 1) * tk, n_ctx), tk), :]

    def vt_chunk(j):
        return jnp.concatenate([vt_ref[0, 0, 1 + (j - 1) * units + u] for u in range(units)], axis=1)

    scores_softmax(k_ref[0, 0:n_ctx, :], 0)
    if n_lat_chunks == 0:
        values(vt_ref[0, 0, 0], 0)
    else:
        assert n_lat_chunks % 2 == 0
        scores_softmax(k_chunk(1), 1)
        values(vt_ref[0, 0, 0], 0)

        def two_steps(i, carry):
            t = 2 * i + 2
            scores_softmax(k_chunk(t), 0)
            values(vt_chunk(t - 1), 1)
            scores_softmax(k_chunk(t + 1), 1)
            values(vt_chunk(t), 0)
            return carry

        lax.fori_loop(0, (n_lat_chunks - 2) // 2, two_steps, 0)
        scores_softmax(k_chunk(n_lat_chunks), 0)
        values(vt_chunk(n_lat_chunks - 1), 1)
        values(vt_chunk(n_lat_chunks), 0)

    lv = lamv_ref[...]
    lam = (jnp.exp(jnp.sum(lv[0:1] * lv[1:2], axis=1, keepdims=True))
           - jnp.exp(jnp.sum(lv[2:3] * lv[3:4], axis=1, keepdims=True)) + lam_init)
    l0 = jnp.sum(l_scr[0], axis=0, keepdims=True)
    l1 = jnp.sum(l_scr[1], axis=0, keepdims=True)
    ot = acc_scr[0] / l0 - lam * (acc_scr[1] / l1)
    o = ot.T
    o = o * lax.rsqrt(jnp.mean(o * o, axis=-1, keepdims=True) + EPS)
    o_ref[0] = (o * hg_ref[...] * (1.0 - lam_init)).astype(o_ref.dtype)


def _with_prev(kernel_fn, n_in):
    def wrapped(*refs, **kw):
        kernel_fn(*refs[:n_in], *refs[n_in + 1:], **kw)
    return wrapped


def _attention_calls(kernel_fn, lead, lead_specs, scratch_fn, q, k, vt, n_ctx, kw, name):
    b, n, dq = q.shape
    heads = dq // V_DIM
    tq = n_ctx
    s = n - n_ctx
    n_in = len(lead) + 3
    o = pl.pallas_call(
        functools.partial(kernel_fn, n_lat_chunks=s // kw["tk"], **kw),
        grid=(b, heads, s // tq),
        in_specs=lead_specs + [
            pl.BlockSpec((1, tq, V_DIM), lambda bi, h, i: (bi, i + 1, h)),
            pl.BlockSpec((1, n, V_DIM), lambda bi, h, i: (bi, 0, h)),
            pl.BlockSpec((1, 1, n // n_ctx, V_DIM, n_ctx), lambda bi, h, i: (bi, h, 0, 0, 0))],
        out_specs=pl.BlockSpec((1, tq, V_DIM), lambda bi, h, i: (bi, i + 1, h)),
        out_shape=jax.ShapeDtypeStruct((b, n, dq), BF16),
        scratch_shapes=scratch_fn(tq),
        compiler_params=_cparams(("parallel", "parallel", "arbitrary")),
        name=name,
    )(*lead, q, k, vt)
    return pl.pallas_call(
        functools.partial(_with_prev(kernel_fn, n_in), n_lat_chunks=0, **kw),
        grid=(b, heads, 1),
        in_specs=lead_specs + [
            pl.BlockSpec((1, tq, V_DIM), lambda bi, h, i: (bi, 0, h)),
            pl.BlockSpec((1, n_ctx, V_DIM), lambda bi, h, i: (bi, 0, h)),
            pl.BlockSpec((1, 1, 1, V_DIM, n_ctx), lambda bi, h, i: (bi, h, 0, 0, 0)),
            pl.BlockSpec(memory_space=pl.ANY)],
        out_specs=pl.BlockSpec((1, tq, V_DIM), lambda bi, h, i: (bi, 0, h)),
        out_shape=jax.ShapeDtypeStruct((b, n, dq), BF16),
        scratch_shapes=scratch_fn(tq),
        input_output_aliases={n_in: 0},
        compiler_params=_cparams(("parallel", "parallel", "arbitrary")),
        name=name + "_ctx",
    )(*lead, q, k, vt, o)


def _row_norm_max(x, heads):
    b, n, _ = x.shape
    xf = x.astype(F32).reshape(b, n, heads, HEAD_DIM, 2)
    return jnp.sqrt(jnp.max(jnp.sum(xf * xf, axis=3), axis=1))


def _attention(lamv, head_g, q, k, vt, n_ctx, lam_init):
    b, n, dq = q.shape
    heads = dq // V_DIM
    s = n - n_ctx
    tk = _pick_tile(s // 2, 1024, n_ctx)
    rb = _pick_tile(n_ctx, 128, 8)
    kw = dict(n_ctx=n_ctx, tk=tk, lam_init=lam_init, rb=rb)
    spec2 = [pl.BlockSpec((4, HEAD_DIM), lambda bi, h, i: (0, 0)),
             pl.BlockSpec((1, V_DIM), lambda bi, h, i: (0, 0))]

    def robust_scratch(tq):
        return [pltpu.VMEM((2, 8, tq), F32), pltpu.VMEM((2, 8, tq), F32),
                pltpu.VMEM((2, V_DIM, tq), F32), pltpu.VMEM((2, 2, 8, tq), F32),
                pltpu.VMEM((2, 2, 8, tq), F32), pltpu.VMEM((2, 2, tk, tq), F32),
                pltpu.VMEM((2, 2, tk, tq), BF16)]

    def fast_scratch(tq):
        return [pltpu.VMEM((2, 8, tq), F32), pltpu.VMEM((2, V_DIM, tq), F32),
                pltpu.VMEM((2, 2, tk, tq), BF16)]

    qmax = _row_norm_max(q, heads)
    kmax = _row_norm_max(k, heads)
    kb = jnp.broadcast_to(kmax[:, :, :, None, None], (b, heads, 2, 8, V7X_LANES))
    kb_spec = pl.BlockSpec((1, 1, 2, 8, V7X_LANES), lambda bi, h, i: (bi, h, 0, 0, 0))

    def fast(args):
        return _attention_calls(_attn_fast_kernel, [kb, lamv, head_g], [kb_spec] + spec2, fast_scratch,
                                *args, n_ctx, kw, "diff_attn")

    def robust(args):
        return _attention_calls(_attn_kernel, [lamv, head_g], spec2, robust_scratch,
                                *args, n_ctx, kw, "diff_attn_online")

    safe = jnp.max(qmax * kmax) * BOUND_MARGIN <= SAFE_SCORE_BOUND
    return lax.cond(safe, fast, robust, (q, k, vt))


def _dft_cos_sin(n, rows=None, cols=None, mod=None):
    mod = n if mod is None else mod
    r = jnp.arange(n if rows is None else rows, dtype=jnp.int32)[:, None]
    c = jnp.arange(n if cols is None else cols, dtype=jnp.int32)[None, :]
    ang = ((r * c) % mod).astype(F32) * (2.0 * math.pi / mod)
    return jnp.cos(ang), jnp.sin(ang)


def _channel_dft(x, cs_ref, groups):
    cg = x.shape[1] // groups
    wr, wi = [], []
    for g in range(groups):
        w = jnp.dot(x[:, g * cg:(g + 1) * cg], cs_ref[...], preferred_element_type=F32)
        wr.append(w[:, :cg])
        wi.append(w[:, cg:])
    return jnp.concatenate(wr, axis=1), jnp.concatenate(wi, axis=1)


def _fourier1_kernel(x_ref, cs_ref, kc_ref, ks_ref, tc_ref, ts_ref, zr_ref, zi_ref, *, groups):
    ls, tf, c = x_ref.shape[1:]
    x = x_ref[0].reshape(ls * tf, c)
    wr, wi = _channel_dft(x, cs_ref, groups)
    wr = wr.astype(BF16)
    wi = wi.astype(BF16)
    kc = kc_ref[...]
    ks = ks_ref[...]
    zr = jnp.dot(kc, wr, preferred_element_type=F32) + jnp.dot(ks, wi, preferred_element_type=F32)
    zi = jnp.dot(kc, wi, preferred_element_type=F32) - jnp.dot(ks, wr, preferred_element_type=F32)
    reps = c // V7X_LANES
    tc = jnp.concatenate([tc_ref[0]] * reps, axis=1)
    ts = jnp.concatenate([ts_ref[0]] * reps, axis=1)
    zr_ref[0] = (zr * tc + zi * ts).astype(BF16).reshape(ls, tf, c)
    zi_ref[0] = (zi * tc - zr * ts).astype(BF16).reshape(ls, tf, c)


def _fourier2_kernel(zr_ref, zi_ref, fc_ref, fs_ref, y_ref, *, scale):
    for u in range(zr_ref.shape[1]):
        y = (jnp.dot(fc_ref[...], zr_ref[0, u], preferred_element_type=F32)
             + jnp.dot(fs_ref[...], zi_ref[0, u], preferred_element_type=F32))
        y_ref[0, u] = (y * scale).astype(BF16)


def _fourier_ctx_kernel(x_ref, cs_ref, pc_ref, ps_ref, y_ref, *, groups, scale):
    wr, wi = _channel_dft(x_ref[0], cs_ref, groups)
    y = (jnp.dot(pc_ref[...], wr.astype(BF16), preferred_element_type=F32)
         + jnp.dot(ps_ref[...], wi.astype(BF16), preferred_element_type=F32))
    y_ref[0] = (y * scale).astype(BF16)


def _fourier_tables(s, c):
    cg = c // N_FOURIER_GROUPS
    cc, sc = _dft_cos_sin(cg)
    cs = jnp.concatenate([cc, -sc], axis=1).astype(BF16)
    return cg, cs


def _fourier_latent(fx, groups):
    b, s, c = fx.shape
    cg, cs = _fourier_tables(s, c)
    ls = FOURIER_SLOW
    lf = s // ls
    tf = V7X_BF16_SUBLANES
    assert ls * lf == s and lf % tf == 0
    rows = ls * tf
    pc, ps = _dft_cos_sin(ls)
    eye = jnp.eye(tf, dtype=F32)
    kc = jnp.kron(pc, eye).astype(BF16)
    ks = jnp.kron(ps, eye).astype(BF16)
    u_idx = jnp.repeat(jnp.arange(ls, dtype=jnp.int32), tf)[None, :]
    f_idx = (jnp.arange(lf // tf, dtype=jnp.int32)[:, None] * tf
             + jnp.tile(jnp.arange(tf, dtype=jnp.int32), ls)[None, :])
    ang = ((u_idx * f_idx) % s).astype(F32) * (2.0 * math.pi / s)
    tc = jnp.broadcast_to(jnp.cos(ang)[:, :, None], (lf // tf, rows, V7X_LANES))
    ts = jnp.broadcast_to(jnp.sin(ang)[:, :, None], (lf // tf, rows, V7X_LANES))

    x4 = fx.reshape(b, ls, lf, c)
    zshape = jax.ShapeDtypeStruct((b, ls, lf, c), BF16)
    blk = pl.BlockSpec((1, ls, tf, c), lambda bi, i: (bi, 0, i, 0))
    zr, zi = pl.pallas_call(
        functools.partial(_fourier1_kernel, groups=groups),
        grid=(b, lf // tf),
        in_specs=[blk,
                  pl.BlockSpec((cg, 2 * cg), lambda bi, i: (0, 0)),
                  pl.BlockSpec((rows, rows), lambda bi, i: (0, 0)),
                  pl.BlockSpec((rows, rows), lambda bi, i: (0, 0)),
                  pl.BlockSpec((1, rows, V7X_LANES), lambda bi, i: (i, 0, 0)),
                  pl.BlockSpec((1, rows, V7X_LANES), lambda bi, i: (i, 0, 0))],
        out_specs=[blk, blk],
        out_shape=[zshape, zshape],
        compiler_params=_cparams(("parallel", "parallel")),
        name="fourier_slow",
    )(x4, cs, kc, ks, tc, ts)

    fc, fs = _dft_cos_sin(lf)
    tu = _pick_tile(ls, 4, 1)
    blk2 = pl.BlockSpec((1, tu, lf, c), lambda bi, i: (bi, i, 0, 0))
    yt = pl.pallas_call(
        functools.partial(_fourier2_kernel, scale=1.0 / math.sqrt(s * cg)),
        grid=(b, ls // tu),
        in_specs=[blk2, blk2,
                  pl.BlockSpec((lf, lf), lambda bi, i: (0, 0)),
                  pl.BlockSpec((lf, lf), lambda bi, i: (0, 0))],
        out_specs=blk2,
        out_shape=zshape,
        compiler_params=_cparams(("parallel", "parallel")),
        name="fourier_fast",
    )(zr, zi, fc.astype(BF16), fs.astype(BF16))
    return jnp.swapaxes(yt, 1, 2).reshape(b, s, c)


def _fourier_ctx(fxc, groups):
    b, n_ctx, c = fxc.shape
    cg, cs = _fourier_tables(n_ctx, c)
    pc, ps = _dft_cos_sin(n_ctx)
    return pl.pallas_call(
        functools.partial(_fourier_ctx_kernel, groups=groups, scale=1.0 / math.sqrt(n_ctx * cg)),
        grid=(b,),
        in_specs=[pl.BlockSpec((1, n_ctx, c), lambda bi: (bi, 0, 0)),
                  pl.BlockSpec((cg, 2 * cg), lambda bi: (0, 0)),
                  pl.BlockSpec((n_ctx, n_ctx), lambda bi: (0, 0)),
                  pl.BlockSpec((n_ctx, n_ctx), lambda bi: (0, 0))],
        out_specs=pl.BlockSpec((1, n_ctx, c), lambda bi: (bi, 0, 0)),
        out_shape=jax.ShapeDtypeStruct((b, n_ctx, c), BF16),
        compiler_params=_cparams(("parallel",)),
        name="fourier_ctx",
    )(fxc, cs, pc.astype(BF16), ps.astype(BF16))


def _out_ab_kernel(x_ref, f_ref, o_ref_in, mod_ref, w_ref, out_ref, *, n_ctx, tm, c):
    (gate,) = _row_mods(mod_ref, pl.program_id(1) * tm, tm, n_ctx, (2,))
    y = (jnp.dot(f_ref[0], w_ref[0:c, :], preferred_element_type=F32)
         + jnp.dot(o_ref_in[0], w_ref[c:, :], preferred_element_type=F32))
    out_ref[0] = x_ref[0] + gate * y


def _out_ab(xc, f, o, mods, w_out, n_ctx):
    b, n, d = xc.shape
    c = f.shape[2]
    tm = _pick_tile(n, 640, 128)
    return pl.pallas_call(
        functools.partial(_out_ab_kernel, n_ctx=n_ctx, tm=tm, c=c),
        grid=(b, n // tm),
        in_specs=[pl.BlockSpec((1, tm, d), lambda bi, i: (bi, i, 0)),
                  pl.BlockSpec((1, tm, c), lambda bi, i: (bi, i, 0)),
                  pl.BlockSpec((1, tm, o.shape[2]), lambda bi, i: (bi, i, 0)),
                  pl.BlockSpec((1, 2, 6, d), lambda bi, i: (bi, 0, 0, 0)),
                  pl.BlockSpec(w_out.shape, lambda bi, i: (0, 0))],
        out_specs=pl.BlockSpec((1, tm, d), lambda bi, i: (bi, i, 0)),
        out_shape=jax.ShapeDtypeStruct((b, n, d), F32),
        compiler_params=_cparams(("parallel", "parallel")),
        name="out_proj_attn",
    )(xc, f, o, mods, w_out)


def _out_lru_kernel(x_ref, yf_ref, yr_ref, gate_ref, mod_ref, w_ref, out_ref, *, n_ctx, tm):
    (gate,) = _row_mods(mod_ref, pl.program_id(1) * tm, tm, n_ctx, (2,))
    z = (yf_ref[0] + yr_ref[0]) * _gelu_tanh(gate_ref[0].astype(F32))
    y = jnp.dot(z.astype(BF16), w_ref[...], preferred_element_type=F32)
    out_ref[0] = x_ref[0] + gate * y


def _out_lru(xc, yf, yr, u, mods, w_out, n_ctx):
    b, n, d = xc.shape
    tm = _pick_tile(n, 320, 64)
    row = pl.BlockSpec((1, tm, d), lambda bi, i: (bi, i, 0))
    return pl.pallas_call(
        functools.partial(_out_lru_kernel, n_ctx=n_ctx, tm=tm),
        grid=(b, n // tm),
        in_specs=[row, row, row, row,
                  pl.BlockSpec((1, 2, 6, d), lambda bi, i: (bi, 0, 0, 0)),
                  pl.BlockSpec(w_out.shape, lambda bi, i: (0, 0))],
        out_specs=row,
        out_shape=jax.ShapeDtypeStruct((b, n, d), F32),
        compiler_params=_cparams(("parallel", "parallel")),
        name="out_proj_lru",
    )(xc, yf, yr, u, mods, w_out)


def _conv_kernel(main_ref, prev_ref, next_ref, w_ref, b_ref, o_ref, *, n_blocks, halo):
    i = pl.program_id(1)
    tb = main_ref.shape[1]
    first = jnp.logical_or(i == 0, i == 1)
    last = jnp.logical_or(i == 0, i == n_blocks - 1)
    main = main_ref[0].astype(F32)
    prev = jnp.where(first, 0.0, prev_ref[0].astype(F32))
    nxt = jnp.where(last, 0.0, next_ref[0].astype(F32))
    ext = jnp.concatenate([prev, main, nxt], axis=0)
    w = w_ref[...]
    acc = jnp.broadcast_to(b_ref[...], main.shape)
    for k in range(w.shape[0]):
        off = halo + k - 2
        acc = acc + w[k:k + 1, :] * ext[off:off + tb, :]
    o_ref[0] = acc


def _conv(u, conv_w, conv_b, n_ctx, col):
    b, n, d2 = u.shape
    d = d2 // 2
    tb = n_ctx
    halo = V7X_BF16_SUBLANES
    nb = n // tb
    hb = tb // halo
    return pl.pallas_call(
        functools.partial(_conv_kernel, n_blocks=nb, halo=halo),
        grid=(b, nb),
        in_specs=[pl.BlockSpec((1, tb, d), lambda bi, i: (bi, i, col)),
                  pl.BlockSpec((1, halo, d), lambda bi, i: (bi, jnp.maximum(i * hb - 1, 0), col)),
                  pl.BlockSpec((1, halo, d), lambda bi, i: (bi, jnp.minimum((i + 1) * hb, n // halo - 1), col)),
                  pl.BlockSpec(conv_w.shape, lambda bi, i: (0, 0)),
                  pl.BlockSpec((1, d), lambda bi, i: (0, 0))],
        out_specs=pl.BlockSpec((1, tb, d), lambda bi, i: (bi, i, 0)),
        out_shape=jax.ShapeDtypeStruct((b, n, d), F32),
        compiler_params=_cparams(("parallel", "parallel")),
        name="lru_conv",
    )(u, u, u, conv_w, conv_b.reshape(1, d))


def _lru_scan_kernel(xf_ref, xr_ref, gaw_ref, gab_ref, gxw_ref, gxb_ref, lam_ref, yf_ref, yr_ref,
                     h_scr, a_scr, b_scr):
    t = pl.program_id(1)
    tb, d = xf_ref.shape[1:]
    nblk = gaw_ref.shape[1]
    bs = d // nblk

    @pl.when(t == 0)
    def _():
        h_scr[...] = jnp.zeros(h_scr.shape, F32)

    def block_diag(xb, w_ref, dr):
        return jnp.concatenate(
            [jnp.dot(xb[:, n * bs:(n + 1) * bs], w_ref[dr, n], preferred_element_type=F32)
             for n in range(nblk)], axis=1)

    for dr, x_ref in enumerate((xf_ref, xr_ref)):
        xc = x_ref[0]
        xb = xc.astype(BF16)
        r = jax.nn.sigmoid(block_diag(xb, gaw_ref, dr) + gab_ref[dr:dr + 1, :])
        gi = jax.nn.sigmoid(block_diag(xb, gxw_ref, dr) + gxb_ref[dr:dr + 1, :])
        z = -lam_ref[dr:dr + 1, :]
        softplus = jnp.maximum(z, 0.0) + jnp.log1p(jnp.exp(-jnp.abs(z)))
        a = jnp.exp((-LRU_C) * r * softplus)
        a_scr[dr] = a
        b_scr[dr] = jnp.sqrt(1.0 - a * a) * gi * xc

    row = lax.broadcasted_iota(jnp.int32, (8, d), 0)
    nchunks = tb // 8

    def chunk(c, carry):
        r0 = pl.multiple_of(c * 8, 8)
        a = a_scr[0, pl.ds(r0, 8), :]
        bv = b_scr[0, pl.ds(r0, 8), :]
        for sh in (1, 2, 4):
            ok = row >= sh
            bv = jnp.where(ok, a * pltpu.roll(bv, sh, axis=0) + bv, bv)
            a = jnp.where(ok, a * pltpu.roll(a, sh, axis=0), a)
        hf = bv + a * h_scr[0]
        yf_ref[0, pl.ds(r0, 8), :] = hf
        h_scr[0] = hf[7:8, :]
        r1 = pl.multiple_of((nchunks - 1 - c) * 8, 8)
        a = a_scr[1, pl.ds(r1, 8), :]
        bv = b_scr[1, pl.ds(r1, 8), :]
        for sh in (1, 2, 4):
            ok = row < 8 - sh
            bv = jnp.where(ok, a * pltpu.roll(bv, 8 - sh, axis=0) + bv, bv)
            a = jnp.where(ok, a * pltpu.roll(a, 8 - sh, axis=0), a)
        hr = bv + a * h_scr[1]
        yr_ref[0, pl.ds(r1, 8), :] = hr
        h_scr[1] = hr[0:1, :]
        return carry

    lax.fori_loop(0, nchunks, chunk, 0)


def _lru_scan(xcv, ga_w, ga_b, gx_w, gx_b, lam, n_ctx):
    b, n, d = xcv.shape
    tb = n_ctx
    nt = n // tb
    fwd = pl.BlockSpec((1, tb, d), lambda bi, t: (bi, t, 0))
    rev = pl.BlockSpec((1, tb, d), lambda bi, t: (bi, jnp.where(t == 0, 0, nt - t), 0))
    wspec = pl.BlockSpec(ga_w.shape, lambda bi, t: (0, 0, 0, 0))
    vspec = pl.BlockSpec((2, d), lambda bi, t: (0, 0))
    out = jax.ShapeDtypeStruct((b, n, d), F32)
    return pl.pallas_call(
        _lru_scan_kernel,
        grid=(b, nt),
        in_specs=[fwd, rev, wspec, vspec, wspec, vspec, vspec],
        out_specs=[fwd, rev],
        out_shape=[out, out],
        scratch_shapes=[pltpu.VMEM((2, 1, d), F32), pltpu.VMEM((2, tb, d), F32),
                        pltpu.VMEM((2, tb, d), F32)],
        compiler_params=_cparams(("parallel", "arbitrary")),
        name="lru_scan",
    )(xcv, xcv, ga_w, ga_b, gx_w, gx_b, lam)


def _ffn_kernel(x_ref, mod_ref, g_ref, wg_ref, wu_ref, wd_ref, o_ref, h_scr, *, n_ctx, tm):
    k = pl.program_id(2)
    row0 = pl.program_id(1) * tm

    @pl.when(k == 0)
    def _():
        _norm_mod_into(h_scr, x_ref[0], g_ref[...], mod_ref, row0, n_ctx, 3, 4)
        o_ref[0] = jnp.zeros(o_ref.shape[1:], F32)

    h = h_scr[...]
    act = _silu(jnp.dot(h, wg_ref[...], preferred_element_type=F32)) * jnp.dot(
        h, wu_ref[...], preferred_element_type=F32)
    o_ref[0] += jnp.dot(act.astype(BF16), wd_ref[...], preferred_element_type=F32)

    @pl.when(k == pl.num_programs(2) - 1)
    def _():
        (gate,) = _row_mods(mod_ref, row0, tm, n_ctx, (5,))
        o_ref[0] = x_ref[0] + gate * o_ref[0]


def _ffn(xc, mods, g, wg, wu, wd, n_ctx):
    b, n, d = xc.shape
    dff = wg.shape[1]
    tm = _pick_tile(n, 640, 128)
    tf = _pick_tile(dff, 512, V7X_LANES)
    return pl.pallas_call(
        functools.partial(_ffn_kernel, n_ctx=n_ctx, tm=tm),
        grid=(b, n // tm, dff // tf),
        in_specs=[pl.BlockSpec((1, tm, d), lambda bi, i, k: (bi, i, 0)),
                  pl.BlockSpec((1, 2, 6, d), lambda bi, i, k: (bi, 0, 0, 0)),
                  pl.BlockSpec((1, d), lambda bi, i, k: (0, 0)),
                  pl.BlockSpec((d, tf), lambda bi, i, k: (0, k)),
                  pl.BlockSpec((d, tf), lambda bi, i, k: (0, k)),
                  pl.BlockSpec((tf, d), lambda bi, i, k: (k, 0))],
        out_specs=pl.BlockSpec((1, tm, d), lambda bi, i, k: (bi, i, 0)),
        out_shape=jax.ShapeDtypeStruct((b, n, d), F32),
        scratch_shapes=[pltpu.VMEM((tm, d), BF16)],
        compiler_params=_cparams(("parallel", "parallel", "arbitrary")),
        name="ffn",
    )(xc, mods, g.reshape(1, d), wg, wu, wd)


def _qk_layout(heads):
    perm = np.zeros(heads * V_DIM, np.int32)
    gain_idx = np.zeros(V_DIM, np.int32)
    for h in range(heads):
        for half in range(2):
            for j in range(HEAD_DIM // 2):
                for mp in range(2):
                    new = h * V_DIM + half * HEAD_DIM + 2 * j + mp
                    perm[new] = h * V_DIM + mp * HEAD_DIM + 2 * j + half
                    gain_idx[half * HEAD_DIM + 2 * j + mp] = 2 * j + half
    return perm, gain_idx


def _rope_tables(n_ctx, s):
    n_rows = s // GRID_W
    rows = jnp.repeat(jnp.arange(n_rows, dtype=F32), GRID_W, total_repeat_length=s)
    cols = jnp.tile(jnp.arange(GRID_W, dtype=F32), n_rows)
    n_axis = HEAD_DIM // 4
    inv = ROPE_THETA ** (-jnp.arange(n_axis, dtype=F32) / n_axis)
    ang = jnp.concatenate([rows[:, None] * inv, cols[:, None] * inv], axis=-1)
    cos = jnp.concatenate([jnp.ones((n_ctx, HEAD_DIM // 2), F32), jnp.cos(ang)], axis=0)
    sin = jnp.concatenate([jnp.zeros((n_ctx, HEAD_DIM // 2), F32), jnp.sin(ang)], axis=0)
    cos_h = jnp.repeat(cos, 2, axis=1)
    sin_h = jnp.repeat(sin, 2, axis=1)
    return jnp.concatenate([cos_h, cos_h], axis=1), jnp.concatenate([-sin_h, sin_h], axis=1)


def kernel(x, c, ctx, c_ctx, ada_w, ada_b, norm_mix_g, norm_ffn_g, ffn_w_gate, ffn_w_up, ffn_w_down, ab_w_in, ab_w_out, ab_q_norm_g, ab_k_norm_g, ab_lam_q1, ab_lam_k1, ab_lam_q2, ab_lam_k2, ab_head_norm_g, lru_w_in, lru_w_out, lru_conv_w, lru_conv_b, lru_gate_a_w, lru_gate_a_b, lru_gate_x_w, lru_gate_x_b, lru_lambda):
    b, s, d = x.shape
    n_ctx = ctx.shape[1]
    depth = ada_w.shape[0]
    c_f = d // 2
    heads = (d // 2) // V_DIM
    dq = heads * V_DIM

    cond8 = jnp.zeros((8, d), F32).at[:b].set(c).at[b].set(c_ctx)
    ada = _adaln(cond8, ada_w, ada_b).reshape(depth, 8, 6, d)
    mods_all = jnp.stack([jnp.broadcast_to(ada[:, b][:, None], (depth, b, 6, d)), ada[:, :b]], axis=2)

    perm, gain_idx = _qk_layout(heads)
    cos_t, sin_t = _rope_tables(n_ctx, s)

    xc = jnp.concatenate([ctx, x], axis=1)

    for l in range(depth):
        j = l // 2
        mods = mods_all[l]
        if l % 2 == 0:
            lam_init = 0.8 - 0.6 * math.exp(-0.3 * l)
            w_in = ab_w_in[j]
            w_in = jnp.concatenate([w_in[:, :c_f],
                                    jnp.take(w_in[:, c_f:c_f + dq], perm, axis=1),
                                    jnp.take(w_in[:, c_f + dq:c_f + 2 * dq], perm, axis=1),
                                    w_in[:, c_f + 2 * dq:]], axis=1).astype(BF16)
            u = _proj(xc, mods, norm_mix_g[l], w_in, n_ctx)
            gq = jnp.take(ab_q_norm_g[j], gain_idx).reshape(1, V_DIM)
            gk = jnp.take(ab_k_norm_g[j], gain_idx).reshape(1, V_DIM)
            q, k = _qk_prep(u, cos_t, sin_t, gq, gk, heads, c_f // dq, c_f // dq + 1)
            v = u[:, :, c_f + 2 * dq:]
            vt = jnp.transpose(v.reshape(b, (n_ctx + s) // n_ctx, n_ctx, heads, V_DIM), (0, 3, 1, 4, 2))
            lamv = jnp.stack([ab_lam_q1[j], ab_lam_k1[j], ab_lam_q2[j], ab_lam_k2[j]])
            o = _attention(lamv, ab_head_norm_g[j].reshape(1, V_DIM), q, k, vt, n_ctx, lam_init)
            f_ctx = _fourier_ctx(u[:, :n_ctx, :c_f], N_FOURIER_GROUPS)
            f_lat = _fourier_latent(u[:, n_ctx:, :c_f], N_FOURIER_GROUPS)
            f = jnp.concatenate([f_ctx, f_lat], axis=1)
            xc = _out_ab(xc, f, o, mods, ab_w_out[j].astype(BF16), n_ctx)
        else:
            u = _proj(xc, mods, norm_mix_g[l], lru_w_in[j].astype(BF16), n_ctx)
            xcv = _conv(u, lru_conv_w[j], lru_conv_b[j], n_ctx, 1)
            yf, yr = _lru_scan(xcv, lru_gate_a_w[j].astype(BF16), lru_gate_a_b[j],
                               lru_gate_x_w[j].astype(BF16), lru_gate_x_b[j], lru_lambda[j], n_ctx)
            xc = _out_lru(xc, yf, yr, u, mods, lru_w_out[j].astype(BF16), n_ctx)
        xc = _ffn(xc, mods, norm_ffn_g[l], ffn_w_gate[l].astype(BF16), ffn_w_up[l].astype(BF16),
                  ffn_w_down[l].astype(BF16), n_ctx)
    return xc[:, n_ctx:]
```

```python
import functools
import math

import jax
import jax.numpy as jnp
import numpy as np
from jax import lax
from jax.experimental import pallas as pl
from jax.experimental.pallas import tpu as pltpu

F32 = jnp.float32
BF16 = jnp.bfloat16

EPS = 1e-6
GRID_W = 64
ROPE_THETA = 10000.0
HEAD_DIM = 64
V_DIM = 2 * HEAD_DIM
N_FOURIER_GROUPS = 4
LRU_C = 8.0
LOG2E = 1.4426950408889634
Q_SCALE = (HEAD_DIM ** -0.5) * LOG2E

V7X_LANES = 128
V7X_BF16_SUBLANES = 16
V7X_VMEM_BYTES = 64 * 1024 * 1024
VMEM_LIMIT = V7X_VMEM_BYTES - 8 * 1024 * 1024

FOURIER_SLOW = 32
FAST_ATTN_UNROLL = 3
NEG_BIG = -1e30
BOUND_MARGIN = 1.02
SAFE_SCORE_BOUND = 50.0


def _cparams(sem, flags=None):
    return pltpu.CompilerParams(dimension_semantics=sem, vmem_limit_bytes=VMEM_LIMIT, flags=flags)


def _pick_tile(n, target, mult):
    best = None
    for t in range(mult, min(n, target) + 1, mult):
        if n % t == 0:
            best = t
    assert best is not None, (n, target, mult)
    return best


def _row_mods(mod_ref, row0, rows, n_ctx, idx):
    r = row0 + lax.broadcasted_iota(jnp.int32, (rows, 1), 0)
    is_ctx = r < n_ctx
    return [jnp.where(is_ctx, mod_ref[0, 0, k:k + 1, :], mod_ref[0, 1, k:k + 1, :]) for k in idx]


def _norm_mod_into(h_scr, x, g, mod_ref, row0, n_ctx, i_shift, i_scale):
    y = x * lax.rsqrt(jnp.mean(x * x, axis=-1, keepdims=True) + EPS)

    @pl.when(row0 >= n_ctx)
    def _():
        a = g * (1.0 + mod_ref[0, 1, i_scale:i_scale + 1, :])
        h_scr[...] = (y * a + mod_ref[0, 1, i_shift:i_shift + 1, :]).astype(BF16)

    @pl.when(row0 < n_ctx)
    def _():
        shift, scale = _row_mods(mod_ref, row0, x.shape[0], n_ctx, (i_shift, i_scale))
        h_scr[...] = ((y * g) * (1.0 + scale) + shift).astype(BF16)


def _silu(x):
    return x * jax.nn.sigmoid(x)


def _gelu_tanh(x):
    return 0.5 * x * (1.0 + jnp.tanh(math.sqrt(2.0 / math.pi) * (x + 0.044715 * (x * x * x))))


def _adaln_kernel(cond_ref, w_ref, b_ref, o_ref):
    s = _silu(cond_ref[...])
    o_ref[0] = jnp.dot(s.astype(BF16), w_ref[0].astype(BF16),
                       preferred_element_type=F32) + b_ref[0]


def _adaln(cond8, ada_w, ada_b):
    depth, d, n6 = ada_w.shape
    tn = _pick_tile(n6, 1024, V7X_LANES)
    return pl.pallas_call(
        _adaln_kernel,
        grid=(depth, n6 // tn),
        in_specs=[pl.BlockSpec((8, d), lambda l, j: (0, 0)),
                  pl.BlockSpec((1, d, tn), lambda l, j: (l, 0, j)),
                  pl.BlockSpec((1, 1, tn), lambda l, j: (l, 0, j))],
        out_specs=pl.BlockSpec((1, 8, tn), lambda l, j: (l, 0, j)),
        out_shape=jax.ShapeDtypeStruct((depth, 8, n6), F32),
        compiler_params=_cparams(("parallel", "parallel")),
        name="adaln",
    )(cond8, ada_w, ada_b.reshape(depth, 1, n6))


def _proj_kernel(x_ref, mod_ref, g_ref, w_ref, o_ref, h_scr, *, n_ctx, tm):
    row0 = pl.program_id(1) * tm

    @pl.when(pl.program_id(2) == 0)
    def _():
        _norm_mod_into(h_scr, x_ref[0], g_ref[...], mod_ref, row0, n_ctx, 0, 1)

    o_ref[0] = jnp.dot(h_scr[...], w_ref[...], preferred_element_type=F32).astype(o_ref.dtype)


def _proj(xc, mods, g, w, n_ctx):
    b, n, d = xc.shape
    nout = w.shape[1]
    tm = _pick_tile(n, 640, 128)
    tn = _pick_tile(nout, 1024, V7X_LANES)
    return pl.pallas_call(
        functools.partial(_proj_kernel, n_ctx=n_ctx, tm=tm),
        grid=(b, n // tm, nout // tn),
        in_specs=[pl.BlockSpec((1, tm, d), lambda bi, i, j: (bi, i, 0)),
                  pl.BlockSpec((1, 2, 6, d), lambda bi, i, j: (bi, 0, 0, 0)),
                  pl.BlockSpec((1, d), lambda bi, i, j: (0, 0)),
                  pl.BlockSpec((d, tn), lambda bi, i, j: (0, j))],
        out_specs=pl.BlockSpec((1, tm, tn), lambda bi, i, j: (bi, i, j)),
        out_shape=jax.ShapeDtypeStruct((b, n, nout), BF16),
        scratch_shapes=[pltpu.VMEM((tm, d), BF16)],
        compiler_params=_cparams(("parallel", "parallel", "arbitrary")),
        name="proj",
    )(xc, mods, g.reshape(1, d), w)


def _qk_prep_kernel(q_ref, k_ref, cos_ref, sin_ref, gq_ref, gk_ref, grp_ref, qo_ref, ko_ref, *, heads, q_scale):
    cos = cos_ref[...]
    sin = sin_ref[...]
    w2 = 2 * V_DIM
    for src, g_ref, dst, sc in ((q_ref, gq_ref, qo_ref, q_scale), (k_ref, gk_ref, ko_ref, 1.0)):
        g = jnp.concatenate([g_ref[...], g_ref[...]], axis=1)
        for hp in range(heads // 2):
            x = src[0, :, hp * w2:(hp + 1) * w2].astype(F32)
            t = x * x
            t_hi = t.astype(BF16)
            t_lo = (t - t_hi.astype(F32)).astype(BF16)
            ss = (jnp.dot(t_hi, grp_ref[...], preferred_element_type=F32)
                  + jnp.dot(t_lo, grp_ref[...], preferred_element_type=F32))
            y = x * lax.rsqrt(ss * (1.0 / HEAD_DIM) + EPS) * g
            for hh in range(2):
                yh = y[:, hh * V_DIM:(hh + 1) * V_DIM]
                yh = yh * cos + pltpu.roll(yh, 64, axis=1) * sin
                c0 = hp * w2 + hh * V_DIM
                dst[0, :, c0:c0 + V_DIM] = (yh * sc).astype(BF16)


def _qk_prep(u, cos_t, sin_t, gq, gk, heads, q_col, k_col):
    b, n, _ = u.shape
    dq = heads * V_DIM
    assert heads % 2 == 0
    tm = _pick_tile(n, 640, 128)
    out = jax.ShapeDtypeStruct((b, n, dq), BF16)
    lane = np.arange(2 * V_DIM)
    group = ((lane[:, None] // V_DIM == lane[None, :] // V_DIM)
             & (lane[:, None] % 2 == lane[None, :] % 2)).astype(np.float32)
    return pl.pallas_call(
        functools.partial(_qk_prep_kernel, heads=heads, q_scale=Q_SCALE),
        grid=(b, n // tm),
        in_specs=[pl.BlockSpec((1, tm, dq), lambda bi, i: (bi, i, q_col)),
                  pl.BlockSpec((1, tm, dq), lambda bi, i: (bi, i, k_col)),
                  pl.BlockSpec((tm, V_DIM), lambda bi, i: (i, 0)),
                  pl.BlockSpec((tm, V_DIM), lambda bi, i: (i, 0)),
                  pl.BlockSpec((1, V_DIM), lambda bi, i: (0, 0)),
                  pl.BlockSpec((1, V_DIM), lambda bi, i: (0, 0)),
                  pl.BlockSpec((2 * V_DIM, 2 * V_DIM), lambda bi, i: (0, 0))],
        out_specs=[pl.BlockSpec((1, tm, dq), lambda bi, i: (bi, i, 0)),
                   pl.BlockSpec((1, tm, dq), lambda bi, i: (bi, i, 0))],
        out_shape=[out, out],
        compiler_params=_cparams(("parallel", "parallel")),
        name="qk_prep",
    )(u, u, cos_t, sin_t, gq, gk, jnp.asarray(group, BF16))


def _attn_kernel(lamv_ref, hg_ref, q_ref, k_ref, vt_ref, o_ref, m_scr, l_scr, acc_scr, cm_scr, alpha_scr,
                 s_scr, p_scr,
                 *, n_ctx, tk, n_lat_chunks, lam_init, rb):
    q = q_ref[0]
    tq = q.shape[0]
    lane = lax.broadcasted_iota(jnp.int32, q.shape, 1)
    even = (lane % 2) == 0
    zero = jnp.zeros_like(q)
    qm = (jnp.where(even, q, zero), jnp.where(even, zero, q))

    m_scr[...] = jnp.full(m_scr.shape, NEG_BIG, F32)
    l_scr[...] = jnp.zeros(l_scr.shape, F32)
    acc_scr[...] = jnp.zeros(acc_scr.shape, F32)

    def scores(kc, sb):
        rows = kc.shape[0]
        for mp in range(2):
            s = lax.dot_general(kc, qm[mp], (((1,), (1,)), ((), ())), preferred_element_type=F32)
            s_scr[sb, mp, 0:rows, :] = s
            cm_scr[sb, mp] = jnp.broadcast_to(jnp.max(s, axis=0, keepdims=True), (8, tq))

    def softmax(sb, rows):
        for mp in range(2):
            m_old = m_scr[mp]
            m_new = jnp.maximum(m_old, cm_scr[sb, mp])
            m_scr[mp] = m_new
            alpha = jnp.exp2(m_old - m_new)
            alpha_scr[sb, mp] = alpha
            lsum = jnp.zeros((8, tq), F32)
            for r0 in range(0, rows, rb):
                p = jnp.exp2(s_scr[sb, mp, r0:r0 + rb, :] - m_new[0:1, :])
                lsum = lsum + jnp.sum(p.reshape(rb // 8, 8, tq), axis=0)
                p_scr[sb, mp, r0:r0 + rb, :] = p.astype(BF16)
            l_scr[mp] = alpha * l_scr[mp] + jnp.broadcast_to(jnp.sum(lsum, axis=0, keepdims=True), (8, tq))

    def values(vtc, sb):
        rows = vtc.shape[1]
        for mp in range(2):
            acc_scr[mp] = alpha_scr[sb, mp, 0:1, :] * acc_scr[mp] + jnp.dot(
                vtc, p_scr[sb, mp, 0:rows, :], preferred_element_type=F32)

    units = tk // n_ctx

    def k_chunk(j):
        return k_ref[0, pl.ds(pl.multiple_of(n_ctx + (j - 1) * tk, n_ctx), tk), :]

    def vt_chunk(j):
        return jnp.concatenate([vt_ref[0, 0, 1 + (j - 1) * units + u] for u in range(units)], axis=1)

    n_chunks = n_lat_chunks + 1
    scores(k_ref[0, 0:n_ctx, :], 0)
    if n_chunks == 1:
        softmax(0, n_ctx)
        values(vt_ref[0, 0, 0], 0)
    else:
        assert n_chunks % 2 == 1
        scores(k_chunk(1), 1)
        softmax(0, n_ctx)
        scores(k_chunk(2), 0)
        softmax(1, tk)
        values(vt_ref[0, 0, 0], 0)

        def two_steps(i, carry):
            t = 2 * i + 1
            softmax(0, tk)
            scores(k_chunk(t + 2), 1)
            values(vt_chunk(t), 1)
            softmax(1, tk)
            scores(k_chunk(t + 3), 0)
            values(vt_chunk(t + 1), 0)
            return carry

        lax.fori_loop(0, (n_chunks - 3) // 2, two_steps, 0)
        softmax(0, tk)
        values(vt_chunk(n_chunks - 2), 1)
        values(vt_chunk(n_chunks - 1), 0)

    lv = lamv_ref[...]
    lam = (jnp.exp(jnp.sum(lv[0:1] * lv[1:2], axis=1, keepdims=True))
           - jnp.exp(jnp.sum(lv[2:3] * lv[3:4], axis=1, keepdims=True)) + lam_init)
    ot = acc_scr[0] / l_scr[0, 0:1, :] - lam * (acc_scr[1] / l_scr[1, 0:1, :])
    o = ot.T
    o = o * lax.rsqrt(jnp.mean(o * o, axis=-1, keepdims=True) + EPS)
    o_ref[0] = (o * hg_ref[...] * (1.0 - lam_init)).astype(o_ref.dtype)


def _attn_fast_kernel(lamv_ref, hg_ref, q_ref, k_ref, vt_ref, o_ref, l_scr, acc_scr, p_scr,
                      *, n_ctx, tk, n_lat_chunks, lam_init, rb, unroll):
    q = q_ref[0]
    tq = q.shape[0]
    lane = lax.broadcasted_iota(jnp.int32, q.shape, 1)
    even = (lane % 2) == 0
    zero = jnp.zeros_like(q)
    qm = (jnp.where(even, q, zero), jnp.where(even, zero, q))
    nt = (((1,), (1,)), ((), ()))
    ref = lamv_ref[4:5, 0:1]

    l_scr[...] = jnp.zeros(l_scr.shape, F32)
    acc_scr[...] = jnp.zeros(acc_scr.shape, F32)

    def scores_softmax(kc, pb):
        rows = kc.shape[0]
        for mp in range(2):
            lsum = jnp.zeros((8, tq), F32)
            for r0 in range(0, rows, rb):
                s = lax.dot_general(kc[r0:r0 + rb, :], qm[mp], nt, preferred_element_type=F32)
                p = jnp.exp2(s - ref)
                lsum = lsum + jnp.sum(p.reshape(rb // 8, 8, tq), axis=0)
                p_scr[pb, mp, r0:r0 + rb, :] = p.astype(BF16)
            l_scr[mp] += lsum

    def values(vtc, pb):
        rows = vtc.shape[1]
        for mp in range(2):
            acc_scr[mp] += jnp.dot(vtc, p_scr[pb, mp, 0:rows, :], preferred_element_type=F32)

    units = tk // n_ctx

    def k_chunk(j):
        return k_ref[0, pl.ds(pl.multiple_of(n_ctx + (j - 1) * tk, n_ctx), tk), :]

    def vt_chunk(j):
        return jnp.concatenate([vt_ref[0, 0, 1 + (j - 1) * units + u] for u in range(units)], axis=1)

    scores_softmax(k_ref[0, 0:n_ctx, :], 0)
    if n_lat_chunks == 0:
        values(vt_ref[0, 0, 0], 0)
    else:
        assert n_lat_chunks % 2 == 0
        scores_softmax(k_chunk(1), 1)
        values(vt_ref[0, 0, 0], 0)

        def two_steps(t):
            scores_softmax(k_chunk(t), 0)
            values(vt_chunk(t - 1), 1)
            scores_softmax(k_chunk(t + 1), 1)
            values(vt_chunk(t), 0)

        n_pairs = (n_lat_chunks - 2) // 2
        n_trips = n_pairs // unroll

        def trip(i, carry):
            for u in range(unroll):
                two_steps(2 + 2 * (i * unroll + u))
            return carry

        if n_trips:
            lax.fori_loop(0, n_trips, trip, 0)
        for pr in range(n_trips * unroll, n_pairs):
            two_steps(2 + 2 * pr)
        scores_softmax(k_chunk(n_lat_chunks), 0)
        values(vt_chunk(n_lat_chunks - 1), 1)
        values(vt_chunk(n_lat_chunks), 0)

    lv = lamv_ref[...]
    lam = (jnp.exp(jnp.sum(lv[0:1] * lv[1:2], axis=1, keepdims=True))
           - jnp.exp(jnp.sum(lv[2:3] * lv[3:4], axis=1, keepdims=True)) + lam_init)
    l0 = jnp.sum(l_scr[0], axis=0, keepdims=True)
    l1 = jnp.sum(l_scr[1], axis=0, keepdims=True)
    ot = acc_scr[0] / l0 - lam * (acc_scr[1] / l1)
    o = ot.T
    o = o * lax.rsqrt(jnp.mean(o * o, axis=-1, keepdims=True) + EPS)
    o_ref[0] = (o * hg_ref[...] * (1.0 - lam_init)).astype(o_ref.dtype)


def _with_prev(kernel_fn, n_in):
    def wrapped(*refs, **kw):
        kernel_fn(*refs[:n_in], *refs[n_in + 1:], **kw)
    return wrapped


def _attention_calls(kernel_fn, lead, lead_specs, scratch_fn, q, k, vt, n_ctx, kw, name):
    b, n, dq = q.shape
    heads = dq // V_DIM
    tq = n_ctx
    s = n - n_ctx
    n_in = len(lead) + 3
    o = pl.pallas_call(
        functools.partial(kernel_fn, n_lat_chunks=s // kw["tk"], **kw),
        grid=(b, heads, s // tq),
        in_specs=lead_specs + [
            pl.BlockSpec((1, tq, V_DIM), lambda bi, h, i: (bi, i + 1, h)),
            pl.BlockSpec((1, n, V_DIM), lambda bi, h, i: (bi, 0, h)),
            pl.BlockSpec((1, 1, n // n_ctx, V_DIM, n_ctx), lambda bi, h, i: (bi, h, 0, 0, 0))],
        out_specs=pl.BlockSpec((1, tq, V_DIM), lambda bi, h, i: (bi, i + 1, h)),
        out_shape=jax.ShapeDtypeStruct((b, n, dq), BF16),
        scratch_shapes=scratch_fn(tq),
        compiler_params=_cparams(("parallel", "parallel", "arbitrary")),
        name=name,
    )(*lead, q, k, vt)
    return pl.pallas_call(
        functools.partial(_with_prev(kernel_fn, n_in), n_lat_chunks=0, **kw),
        grid=(b, heads, 1),
        in_specs=lead_specs + [
            pl.BlockSpec((1, tq, V_DIM), lambda bi, h, i: (bi, 0, h)),
            pl.BlockSpec((1, n_ctx, V_DIM), lambda bi, h, i: (bi, 0, h)),
            pl.BlockSpec((1, 1, 1, V_DIM, n_ctx), lambda bi, h, i: (bi, h, 0, 0, 0)),
            pl.BlockSpec(memory_space=pl.ANY)],
        out_specs=pl.BlockSpec((1, tq, V_DIM), lambda bi, h, i: (bi, 0, h)),
        out_shape=jax.ShapeDtypeStruct((b, n, dq), BF16),
        scratch_shapes=scratch_fn(tq),
        input_output_aliases={n_in: 0},
        compiler_params=_cparams(("parallel", "parallel", "arbitrary")),
        name=name + "_ctx",
    )(*lead, q, k, vt, o)


def _attention(lam_vecs, head_g, q, k, vt, n_ctx, lam_init, bound):
    b, n, dq = q.shape
    s = n - n_ctx
    tk = _pick_tile(s // 2, 1024, n_ctx)
    rb = _pick_tile(n_ctx, 128, 8)
    kw = dict(n_ctx=n_ctx, tk=tk, lam_init=lam_init, rb=rb)
    lamv = jnp.concatenate([lam_vecs, jnp.full((4, HEAD_DIM), bound, F32)], axis=0)
    spec2 = [pl.BlockSpec((8, HEAD_DIM), lambda bi, h, i: (0, 0)),
             pl.BlockSpec((1, V_DIM), lambda bi, h, i: (0, 0))]

    def robust_scratch(tq):
        return [pltpu.VMEM((2, 8, tq), F32), pltpu.VMEM((2, 8, tq), F32),
                pltpu.VMEM((2, V_DIM, tq), F32), pltpu.VMEM((2, 2, 8, tq), F32),
                pltpu.VMEM((2, 2, 8, tq), F32), pltpu.VMEM((2, 2, tk, tq), F32),
                pltpu.VMEM((2, 2, tk, tq), BF16)]

    def fast_scratch(tq):
        return [pltpu.VMEM((2, 8, tq), F32), pltpu.VMEM((2, V_DIM, tq), F32),
                pltpu.VMEM((2, 2, tk, tq), BF16)]

    def fast(args):
        return _attention_calls(_attn_fast_kernel, [lamv, head_g], spec2, fast_scratch,
                                *args, n_ctx, dict(kw, unroll=FAST_ATTN_UNROLL), "diff_attn")

    def robust(args):
        return _attention_calls(_attn_kernel, [lamv, head_g], spec2, robust_scratch,
                                *args, n_ctx, kw, "diff_attn_online")

    return lax.cond(bound <= SAFE_SCORE_BOUND, fast, robust, (q, k, vt))


def _dft_cos_sin(n, rows=None, cols=None, mod=None):
    mod = n if mod is None else mod
    r = jnp.arange(n if rows is None else rows, dtype=jnp.int32)[:, None]
    c = jnp.arange(n if cols is None else cols, dtype=jnp.int32)[None, :]
    ang = ((r * c) % mod).astype(F32) * (2.0 * math.pi / mod)
    return jnp.cos(ang), jnp.sin(ang)


def _channel_dft(x, cs_ref, groups):
    cg = x.shape[1] // groups
    wr, wi = [], []
    for g in range(groups):
        w = jnp.dot(x[:, g * cg:(g + 1) * cg], cs_ref[...], preferred_element_type=F32)
        wr.append(w[:, :cg])
        wi.append(w[:, cg:])
    return jnp.concatenate(wr, axis=1), jnp.concatenate(wi, axis=1)


def _fourier1_kernel(x_ref, cs_ref, kc_ref, ks_ref, tc_ref, ts_ref, zr_ref, zi_ref, *, groups):
    ls, tf, c = x_ref.shape[1:]
    x = x_ref[0].reshape(ls * tf, c)
    wr, wi = _channel_dft(x, cs_ref, groups)
    wr = wr.astype(BF16)
    wi = wi.astype(BF16)
    kc = kc_ref[...]
    ks = ks_ref[...]
    zr = jnp.dot(kc, wr, preferred_element_type=F32) + jnp.dot(ks, wi, preferred_element_type=F32)
    zi = jnp.dot(kc, wi, preferred_element_type=F32) - jnp.dot(ks, wr, preferred_element_type=F32)
    reps = c // V7X_LANES
    tc = jnp.concatenate([tc_ref[0]] * reps, axis=1)
    ts = jnp.concatenate([ts_ref[0]] * reps, axis=1)
    zr_ref[0] = (zr * tc + zi * ts).astype(BF16).reshape(ls, tf, c)
    zi_ref[0] = (zi * tc - zr * ts).astype(BF16).reshape(ls, tf, c)


def _fourier2_kernel(zr_ref, zi_ref, fc_ref, fs_ref, y_ref, *, scale):
    for u in range(zr_ref.shape[1]):
        y = (jnp.dot(fc_ref[...], zr_ref[0, u], preferred_element_type=F32)
             + jnp.dot(fs_ref[...], zi_ref[0, u], preferred_element_type=F32))
        y_ref[0, u] = (y * scale).astype(BF16)


def _fourier_ctx_kernel(x_ref, cs_ref, pc_ref, ps_ref, y_ref, *, groups, scale):
    wr, wi = _channel_dft(x_ref[0], cs_ref, groups)
    y = (jnp.dot(pc_ref[...], wr.astype(BF16), preferred_element_type=F32)
         + jnp.dot(ps_ref[...], wi.astype(BF16), preferred_element_type=F32))
    y_ref[0] = (y * scale).astype(BF16)


def _fourier_tables(s, c):
    cg = c // N_FOURIER_GROUPS
    cc, sc = _dft_cos_sin(cg)
    cs = jnp.concatenate([cc, -sc], axis=1).astype(BF16)
    return cg, cs


def _fourier_latent(fx, groups):
    b, s, c = fx.shape
    cg, cs = _fourier_tables(s, c)
    ls = FOURIER_SLOW
    lf = s // ls
    tf = V7X_BF16_SUBLANES
    assert ls * lf == s and lf % tf == 0
    rows = ls * tf
    pc, ps = _dft_cos_sin(ls)
    eye = jnp.eye(tf, dtype=F32)
    kc = jnp.kron(pc, eye).astype(BF16)
    ks = jnp.kron(ps, eye).astype(BF16)
    u_idx = jnp.repeat(jnp.arange(ls, dtype=jnp.int32), tf)[None, :]
    f_idx = (jnp.arange(lf // tf, dtype=jnp.int32)[:, None] * tf
             + jnp.tile(jnp.arange(tf, dtype=jnp.int32), ls)[None, :])
    ang = ((u_idx * f_idx) % s).astype(F32) * (2.0 * math.pi / s)
    tc = jnp.broadcast_to(jnp.cos(ang)[:, :, None], (lf // tf, rows, V7X_LANES))
    ts = jnp.broadcast_to(jnp.sin(ang)[:, :, None], (lf // tf, rows, V7X_LANES))

    x4 = fx.reshape(b, ls, lf, c)
    zshape = jax.ShapeDtypeStruct((b, ls, lf, c), BF16)
    blk = pl.BlockSpec((1, ls, tf, c), lambda bi, i: (bi, 0, i, 0))
    zr, zi = pl.pallas_call(
        functools.partial(_fourier1_kernel, groups=groups),
        grid=(b, lf // tf),
        in_specs=[blk,
                  pl.BlockSpec((cg, 2 * cg), lambda bi, i: (0, 0)),
                  pl.BlockSpec((rows, rows), lambda bi, i: (0, 0)),
                  pl.BlockSpec((rows, rows), lambda bi, i: (0, 0)),
                  pl.BlockSpec((1, rows, V7X_LANES), lambda bi, i: (i, 0, 0)),
                  pl.BlockSpec((1, rows, V7X_LANES), lambda bi, i: (i, 0, 0))],
        out_specs=[blk, blk],
        out_shape=[zshape, zshape],
        compiler_params=_cparams(("parallel", "parallel")),
        name="fourier_slow",
    )(x4, cs, kc, ks, tc, ts)

    fc, fs = _dft_cos_sin(lf)
    tu = _pick_tile(ls, 4, 1)
    blk2 = pl.BlockSpec((1, tu, lf, c), lambda bi, i: (bi, i, 0, 0))
    yt = pl.pallas_call(
        functools.partial(_fourier2_kernel, scale=1.0 / math.sqrt(s * cg)),
        grid=(b, ls // tu),
        in_specs=[blk2, blk2,
                  pl.BlockSpec((lf, lf), lambda bi, i: (0, 0)),
                  pl.BlockSpec((lf, lf), lambda bi, i: (0, 0))],
        out_specs=blk2,
        out_shape=zshape,
        compiler_params=_cparams(("parallel", "parallel")),
        name="fourier_fast",
    )(zr, zi, fc.astype(BF16), fs.astype(BF16))
    return jnp.swapaxes(yt, 1, 2).reshape(b, s, c)


def _fourier_ctx(fxc, groups):
    b, n_ctx, c = fxc.shape
    cg, cs = _fourier_tables(n_ctx, c)
    pc, ps = _dft_cos_sin(n_ctx)
    return pl.pallas_call(
        functools.partial(_fourier_ctx_kernel, groups=groups, scale=1.0 / math.sqrt(n_ctx * cg)),
        grid=(b,),
        in_specs=[pl.BlockSpec((1, n_ctx, c), lambda bi: (bi, 0, 0)),
                  pl.BlockSpec((cg, 2 * cg), lambda bi: (0, 0)),
                  pl.BlockSpec((n_ctx, n_ctx), lambda bi: (0, 0)),
                  pl.BlockSpec((n_ctx, n_ctx), lambda bi: (0, 0))],
        out_specs=pl.BlockSpec((1, n_ctx, c), lambda bi: (bi, 0, 0)),
        out_shape=jax.ShapeDtypeStruct((b, n_ctx, c), BF16),
        compiler_params=_cparams(("parallel",)),
        name="fourier_ctx",
    )(fxc, cs, pc.astype(BF16), ps.astype(BF16))


def _out_ab_kernel(x_ref, f_ref, o_ref_in, mod_ref, w_ref, out_ref, *, n_ctx, tm, c):
    (gate,) = _row_mods(mod_ref, pl.program_id(1) * tm, tm, n_ctx, (2,))
    y = (jnp.dot(f_ref[0], w_ref[0:c, :], preferred_element_type=F32)
         + jnp.dot(o_ref_in[0], w_ref[c:, :], preferred_element_type=F32))
    out_ref[0] = x_ref[0] + gate * y


def _out_ab(xc, f, o, mods, w_out, n_ctx):
    b, n, d = xc.shape
    c = f.shape[2]
    tm = _pick_tile(n, 640, 128)
    return pl.pallas_call(
        functools.partial(_out_ab_kernel, n_ctx=n_ctx, tm=tm, c=c),
        grid=(b, n // tm),
        in_specs=[pl.BlockSpec((1, tm, d), lambda bi, i: (bi, i, 0)),
                  pl.BlockSpec((1, tm, c), lambda bi, i: (bi, i, 0)),
                  pl.BlockSpec((1, tm, o.shape[2]), lambda bi, i: (bi, i, 0)),
                  pl.BlockSpec((1, 2, 6, d), lambda bi, i: (bi, 0, 0, 0)),
                  pl.BlockSpec(w_out.shape, lambda bi, i: (0, 0))],
        out_specs=pl.BlockSpec((1, tm, d), lambda bi, i: (bi, i, 0)),
        out_shape=jax.ShapeDtypeStruct((b, n, d), F32),
        compiler_params=_cparams(("parallel", "parallel")),
        name="out_proj_attn",
    )(xc, f, o, mods, w_out)


def _out_lru_kernel(x_ref, yf_ref, yr_ref, gate_ref, mod_ref, w_ref, out_ref, *, n_ctx, tm):
    (gate,) = _row_mods(mod_ref, pl.program_id(1) * tm, tm, n_ctx, (2,))
    z = (yf_ref[0] + yr_ref[0]) * _gelu_tanh(gate_ref[0].astype(F32))
    y = jnp.dot(z.astype(BF16), w_ref[...], preferred_element_type=F32)
    out_ref[0] = x_ref[0] + gate * y


def _out_lru(xc, yf, yr, u, mods, w_out, n_ctx):
    b, n, d = xc.shape
    tm = _pick_tile(n, 320, 64)
    row = pl.BlockSpec((1, tm, d), lambda bi, i: (bi, i, 0))
    return pl.pallas_call(
        functools.partial(_out_lru_kernel, n_ctx=n_ctx, tm=tm),
        grid=(b, n // tm),
        in_specs=[row, row, row, row,
                  pl.BlockSpec((1, 2, 6, d), lambda bi, i: (bi, 0, 0, 0)),
                  pl.BlockSpec(w_out.shape, lambda bi, i: (0, 0))],
        out_specs=row,
        out_shape=jax.ShapeDtypeStruct((b, n, d), F32),
        compiler_params=_cparams(("parallel", "parallel")),
        name="out_proj_lru",
    )(xc, yf, yr, u, mods, w_out)


def _conv_kernel(main_ref, prev_ref, next_ref, w_ref, b_ref, o_ref, *, n_blocks, halo):
    i = pl.program_id(1)
    tb = main_ref.shape[1]
    first = jnp.logical_or(i == 0, i == 1)
    last = jnp.logical_or(i == 0, i == n_blocks - 1)
    main = main_ref[0].astype(F32)
    prev = jnp.where(first, 0.0, prev_ref[0].astype(F32))
    nxt = jnp.where(last, 0.0, next_ref[0].astype(F32))
    ext = jnp.concatenate([prev, main, nxt], axis=0)
    w = w_ref[...]
    acc = jnp.broadcast_to(b_ref[...], main.shape)
    for k in range(w.shape[0]):
        off = halo + k - 2
        acc = acc + w[k:k + 1, :] * ext[off:off + tb, :]
    o_ref[0] = acc


def _conv(u, conv_w, conv_b, n_ctx, col):
    b, n, d2 = u.shape
    d = d2 // 2
    tb = n_ctx
    halo = V7X_BF16_SUBLANES
    nb = n // tb
    hb = tb // halo
    return pl.pallas_call(
        functools.partial(_conv_kernel, n_blocks=nb, halo=halo),
        grid=(b, nb),
        in_specs=[pl.BlockSpec((1, tb, d), lambda bi, i: (bi, i, col)),
                  pl.BlockSpec((1, halo, d), lambda bi, i: (bi, jnp.maximum(i * hb - 1, 0), col)),
                  pl.BlockSpec((1, halo, d), lambda bi, i: (bi, jnp.minimum((i + 1) * hb, n // halo - 1), col)),
                  pl.BlockSpec(conv_w.shape, lambda bi, i: (0, 0)),
                  pl.BlockSpec((1, d), lambda bi, i: (0, 0))],
        out_specs=pl.BlockSpec((1, tb, d), lambda bi, i: (bi, i, 0)),
        out_shape=jax.ShapeDtypeStruct((b, n, d), F32),
        compiler_params=_cparams(("parallel", "parallel")),
        name="lru_conv",
    )(u, u, u, conv_w, conv_b.reshape(1, d))


def _lru_scan_kernel(xf_ref, xr_ref, gaw_ref, gab_ref, gxw_ref, gxb_ref, lam_ref, yf_ref, yr_ref,
                     h_scr, a_scr, b_scr):
    t = pl.program_id(1)
    tb, d = xf_ref.shape[1:]
    nblk = gaw_ref.shape[1]
    bs = d // nblk

    @pl.when(t == 0)
    def _():
        h_scr[...] = jnp.zeros(h_scr.shape, F32)

    def block_diag(xb, w_ref, dr):
        return jnp.concatenate(
            [jnp.dot(xb[:, n * bs:(n + 1) * bs], w_ref[dr, n], preferred_element_type=F32)
             for n in range(nblk)], axis=1)

    for dr, x_ref in enumerate((xf_ref, xr_ref)):
        xc = x_ref[0]
        xb = xc.astype(BF16)
        th_a = jnp.tanh(block_diag(xb, gaw_ref, dr) + gab_ref[dr:dr + 1, :])
        th_x = jnp.tanh(block_diag(xb, gxw_ref, dr) + gxb_ref[dr:dr + 1, :])
        z = -lam_ref[dr:dr + 1, :]
        softplus = jnp.maximum(z, 0.0) + jnp.log1p(jnp.exp(-jnp.abs(z)))
        c = (-0.5 * LRU_C) * softplus
        a = jnp.exp(c + c * th_a)
        a_scr[dr] = a
        w = 1.0 - a * a
        mult = jnp.where(w > 0.0, w * lax.rsqrt(w), 0.0)
        b_scr[dr] = mult * (0.5 * xc) * (1.0 + th_x)

    row = lax.broadcasted_iota(jnp.int32, (8, d), 0)
    nchunks = tb // 8

    def chunk(c, carry):
        r0 = pl.multiple_of(c * 8, 8)
        a = a_scr[0, pl.ds(r0, 8), :]
        bv = b_scr[0, pl.ds(r0, 8), :]
        for sh in (1, 2, 4):
            ok = row >= sh
            bv = jnp.where(ok, a * pltpu.roll(bv, sh, axis=0) + bv, bv)
            a = jnp.where(ok, a * pltpu.roll(a, sh, axis=0), a)
        hf = bv + a * h_scr[0]
        yf_ref[0, pl.ds(r0, 8), :] = hf
        h_scr[0] = hf[7:8, :]
        r1 = pl.multiple_of((nchunks - 1 - c) * 8, 8)
        a = a_scr[1, pl.ds(r1, 8), :]
        bv = b_scr[1, pl.ds(r1, 8), :]
        for sh in (1, 2, 4):
            ok = row < 8 - sh
            bv = jnp.where(ok, a * pltpu.roll(bv, 8 - sh, axis=0) + bv, bv)
            a = jnp.where(ok, a * pltpu.roll(a, 8 - sh, axis=0), a)
        hr = bv + a * h_scr[1]
        yr_ref[0, pl.ds(r1, 8), :] = hr
        h_scr[1] = hr[0:1, :]
        return carry

    lax.fori_loop(0, nchunks, chunk, 0)


def _lru_scan(xcv, ga_w, ga_b, gx_w, gx_b, lam, n_ctx):
    b, n, d = xcv.shape
    tb = n_ctx
    nt = n // tb
    fwd = pl.BlockSpec((1, tb, d), lambda bi, t: (bi, t, 0))
    rev = pl.BlockSpec((1, tb, d), lambda bi, t: (bi, jnp.where(t == 0, 0, nt - t), 0))
    wspec = pl.BlockSpec(ga_w.shape, lambda bi, t: (0, 0, 0, 0))
    vspec = pl.BlockSpec((2, d), lambda bi, t: (0, 0))
    out = jax.ShapeDtypeStruct((b, n, d), F32)
    return pl.pallas_call(
        _lru_scan_kernel,
        grid=(b, nt),
        in_specs=[fwd, rev, wspec, vspec, wspec, vspec, vspec],
        out_specs=[fwd, rev],
        out_shape=[out, out],
        scratch_shapes=[pltpu.VMEM((2, 1, d), F32), pltpu.VMEM((2, tb, d), F32),
                        pltpu.VMEM((2, tb, d), F32)],
        compiler_params=_cparams(("parallel", "arbitrary")),
        name="lru_scan",
    )(xcv, xcv, ga_w, ga_b, gx_w, gx_b, lam)


def _ffn_kernel(x_ref, mod_ref, g_ref, wg_ref, wu_ref, wd_ref, o_ref, h_scr, *, n_ctx, tm):
    k = pl.program_id(2)
    row0 = pl.program_id(1) * tm

    @pl.when(k == 0)
    def _():
        _norm_mod_into(h_scr, x_ref[0], g_ref[...], mod_ref, row0, n_ctx, 3, 4)
        o_ref[0] = jnp.zeros(o_ref.shape[1:], F32)

    h = h_scr[...]
    act = _silu(jnp.dot(h, wg_ref[...], preferred_element_type=F32)) * jnp.dot(
        h, wu_ref[...], preferred_element_type=F32)
    o_ref[0] += jnp.dot(act.astype(BF16), wd_ref[...], preferred_element_type=F32)

    @pl.when(k == pl.num_programs(2) - 1)
    def _():
        (gate,) = _row_mods(mod_ref, row0, tm, n_ctx, (5,))
        o_ref[0] = x_ref[0] + gate * o_ref[0]


def _ffn(xc, mods, g, wg, wu, wd, n_ctx):
    b, n, d = xc.shape
    dff = wg.shape[1]
    tm = _pick_tile(n, 640, 128)
    tf = _pick_tile(dff, 512, V7X_LANES)
    return pl.pallas_call(
        functools.partial(_ffn_kernel, n_ctx=n_ctx, tm=tm),
        grid=(b, n // tm, dff // tf),
        in_specs=[pl.BlockSpec((1, tm, d), lambda bi, i, k: (bi, i, 0)),
                  pl.BlockSpec((1, 2, 6, d), lambda bi, i, k: (bi, 0, 0, 0)),
                  pl.BlockSpec((1, d), lambda bi, i, k: (0, 0)),
                  pl.BlockSpec((d, tf), lambda bi, i, k: (0, k)),
                  pl.BlockSpec((d, tf), lambda bi, i, k: (0, k)),
                  pl.BlockSpec((tf, d), lambda bi, i, k: (k, 0))],
        out_specs=pl.BlockSpec((1, tm, d), lambda bi, i, k: (bi, i, 0)),
        out_shape=jax.ShapeDtypeStruct((b, n, d), F32),
        scratch_shapes=[pltpu.VMEM((tm, d), BF16)],
        compiler_params=_cparams(("parallel", "parallel", "arbitrary")),
        name="ffn",
    )(xc, mods, g.reshape(1, d), wg, wu, wd)


def _qk_layout(heads):
    perm = np.zeros(heads * V_DIM, np.int32)
    gain_idx = np.zeros(V_DIM, np.int32)
    for h in range(heads):
        for half in range(2):
            for j in range(HEAD_DIM // 2):
                for mp in range(2):
                    new = h * V_DIM + half * HEAD_DIM + 2 * j + mp
                    perm[new] = h * V_DIM + mp * HEAD_DIM + 2 * j + half
                    gain_idx[half * HEAD_DIM + 2 * j + mp] = 2 * j + half
    return perm, gain_idx


def _rope_tables(n_ctx, s):
    n_rows = s // GRID_W
    rows = jnp.repeat(jnp.arange(n_rows, dtype=F32), GRID_W, total_repeat_length=s)
    cols = jnp.tile(jnp.arange(GRID_W, dtype=F32), n_rows)
    n_axis = HEAD_DIM // 4
    inv = ROPE_THETA ** (-jnp.arange(n_axis, dtype=F32) / n_axis)
    ang = jnp.concatenate([rows[:, None] * inv, cols[:, None] * inv], axis=-1)
    cos = jnp.concatenate([jnp.ones((n_ctx, HEAD_DIM // 2), F32), jnp.cos(ang)], axis=0)
    sin = jnp.concatenate([jnp.zeros((n_ctx, HEAD_DIM // 2), F32), jnp.sin(ang)], axis=0)
    cos_h = jnp.repeat(cos, 2, axis=1)
    sin_h = jnp.repeat(sin, 2, axis=1)
    return jnp.concatenate([cos_h, cos_h], axis=1), jnp.concatenate([-sin_h, sin_h], axis=1)


def kernel(x, c, ctx, c_ctx, ada_w, ada_b, norm_mix_g, norm_ffn_g, ffn_w_gate, ffn_w_up, ffn_w_down, ab_w_in, ab_w_out, ab_q_norm_g, ab_k_norm_g, ab_lam_q1, ab_lam_k1, ab_lam_q2, ab_lam_k2, ab_head_norm_g, lru_w_in, lru_w_out, lru_conv_w, lru_conv_b, lru_gate_a_w, lru_gate_a_b, lru_gate_x_w, lru_gate_x_b, lru_lambda):
    b, s, d = x.shape
    n_ctx = ctx.shape[1]
    depth = ada_w.shape[0]
    c_f = d // 2
    heads = (d // 2) // V_DIM
    dq = heads * V_DIM

    cond8 = jnp.zeros((8, d), F32).at[:b].set(c).at[b].set(c_ctx)
    ada = _adaln(cond8, ada_w, ada_b).reshape(depth, 8, 6, d)
    mods_all = jnp.stack([jnp.broadcast_to(ada[:, b][:, None], (depth, b, 6, d)), ada[:, :b]], axis=2)

    perm, gain_idx = _qk_layout(heads)
    cos_t, sin_t = _rope_tables(n_ctx, s)

    xc = jnp.concatenate([ctx, x], axis=1)

    for l in range(depth):
        j = l // 2
        mods = mods_all[l]
        if l % 2 == 0:
            lam_init = 0.8 - 0.6 * math.exp(-0.3 * l)
            w_in = ab_w_in[j]
            w_in = jnp.concatenate([w_in[:, :c_f],
                                    jnp.take(w_in[:, c_f:c_f + dq], perm, axis=1),
                                    jnp.take(w_in[:, c_f + dq:c_f + 2 * dq], perm, axis=1),
                                    w_in[:, c_f + 2 * dq:]], axis=1).astype(BF16)
            u = _proj(xc, mods, norm_mix_g[l], w_in, n_ctx)
            gq = jnp.take(ab_q_norm_g[j], gain_idx).reshape(1, V_DIM)
            gk = jnp.take(ab_k_norm_g[j], gain_idx).reshape(1, V_DIM)
            q, k = _qk_prep(u, cos_t, sin_t, gq, gk, heads, c_f // dq, c_f // dq + 1)
            v = u[:, :, c_f + 2 * dq:]
            vt = jnp.transpose(v.reshape(b, (n_ctx + s) // n_ctx, n_ctx, heads, V_DIM), (0, 3, 1, 4, 2))
            lamv = jnp.stack([ab_lam_q1[j], ab_lam_k1[j], ab_lam_q2[j], ab_lam_k2[j]])
            bound = (HEAD_DIM * Q_SCALE * BOUND_MARGIN
                     * jnp.max(jnp.abs(ab_q_norm_g[j])) * jnp.max(jnp.abs(ab_k_norm_g[j])))
            o = _attention(lamv, ab_head_norm_g[j].reshape(1, V_DIM), q, k, vt, n_ctx, lam_init, bound)
            f_ctx = _fourier_ctx(u[:, :n_ctx, :c_f], N_FOURIER_GROUPS)
            f_lat = _fourier_latent(u[:, n_ctx:, :c_f], N_FOURIER_GROUPS)
            f = jnp.concatenate([f_ctx, f_lat], axis=1)
            xc = _out_ab(xc, f, o, mods, ab_w_out[j].astype(BF16), n_ctx)
        else:
            u = _proj(xc, mods, norm_mix_g[l], lru_w_in[j].astype(BF16), n_ctx)
            xcv = _conv(u, lru_conv_w[j], lru_conv_b[j], n_ctx, 1)
            yf, yr = _lru_scan(xcv, (0.5 * lru_gate_a_w[j]).astype(BF16), 0.5 * lru_gate_a_b[j],
                               (0.5 * lru_gate_x_w[j]).astype(BF16), 0.5 * lru_gate_x_b[j],
                               lru_lambda[j], n_ctx)
            xc = _out_lru(xc, yf, yr, u, mods, lru_w_out[j].astype(BF16), n_ctx)
        xc = _ffn(xc, mods, norm_ffn_g[l], ffn_w_gate[l].astype(BF16), ffn_w_up[l].astype(BF16),
                  ffn_w_down[l].astype(BF16), n_ctx)
    return xc[:, n_ctx:]
```

```python
import functools
import math

import jax
import jax.numpy as jnp
import numpy as np
from jax import lax
from jax.experimental import pallas as pl
from jax.experimental.pallas import tpu as pltpu

F32 = jnp.float32
BF16 = jnp.bfloat16

EPS = 1e-6
GRID_W = 64
ROPE_THETA = 10000.0
HEAD_DIM = 64
V_DIM = 2 * HEAD_DIM
N_FOURIER_GROUPS = 4
LRU_C = 8.0
LOG2E = 1.4426950408889634
Q_SCALE = (HEAD_DIM ** -0.5) * LOG2E

V7X_LANES = 128
V7X_BF16_SUBLANES = 16
V7X_VMEM_BYTES = 64 * 1024 * 1024
VMEM_LIMIT = V7X_VMEM_BYTES - 8 * 1024 * 1024

FOURIER_SLOW = 32
FAST_ATTN_UNROLL = 3
SCAN_ROW_PITCH = 20
NEG_BIG = -1e30
BOUND_MARGIN = 1.02
SAFE_SCORE_BOUND = 50.0


def _cparams(sem, flags=None):
    return pltpu.CompilerParams(dimension_semantics=sem, vmem_limit_bytes=VMEM_LIMIT, flags=flags)


def _pick_tile(n, target, mult):
    best = None
    for t in range(mult, min(n, target) + 1, mult):
        if n % t == 0:
            best = t
    assert best is not None, (n, target, mult)
    return best


def _row_mods(mod_ref, row0, rows, n_ctx, idx):
    r = row0 + lax.broadcasted_iota(jnp.int32, (rows, 1), 0)
    is_ctx = r < n_ctx
    return [jnp.where(is_ctx, mod_ref[0, 0, k:k + 1, :], mod_ref[0, 1, k:k + 1, :]) for k in idx]


def _norm_mod_into(h_scr, x, g, mod_ref, row0, n_ctx, i_shift, i_scale):
    y = x * lax.rsqrt(jnp.mean(x * x, axis=-1, keepdims=True) + EPS)

    @pl.when(row0 >= n_ctx)
    def _():
        a = g * (1.0 + mod_ref[0, 1, i_scale:i_scale + 1, :])
        h_scr[...] = (y * a + mod_ref[0, 1, i_shift:i_shift + 1, :]).astype(BF16)

    @pl.when(row0 < n_ctx)
    def _():
        shift, scale = _row_mods(mod_ref, row0, x.shape[0], n_ctx, (i_shift, i_scale))
        h_scr[...] = ((y * g) * (1.0 + scale) + shift).astype(BF16)


def _silu(x):
    return x * jax.nn.sigmoid(x)


def _gelu_tanh(x):
    return 0.5 * x * (1.0 + jnp.tanh(math.sqrt(2.0 / math.pi) * (x + 0.044715 * (x * x * x))))


def _adaln_kernel(cond_ref, w_ref, b_ref, o_ref):
    s = _silu(cond_ref[...])
    o_ref[0] = jnp.dot(s.astype(BF16), w_ref[0].astype(BF16),
                       preferred_element_type=F32) + b_ref[0]


def _adaln(cond8, ada_w, ada_b):
    depth, d, n6 = ada_w.shape
    tn = _pick_tile(n6, 1024, V7X_LANES)
    return pl.pallas_call(
        _adaln_kernel,
        grid=(depth, n6 // tn),
        in_specs=[pl.BlockSpec((8, d), lambda l, j: (0, 0)),
                  pl.BlockSpec((1, d, tn), lambda l, j: (l, 0, j)),
                  pl.BlockSpec((1, 1, tn), lambda l, j: (l, 0, j))],
        out_specs=pl.BlockSpec((1, 8, tn), lambda l, j: (l, 0, j)),
        out_shape=jax.ShapeDtypeStruct((depth, 8, n6), F32),
        compiler_params=_cparams(("parallel", "parallel")),
        name="adaln",
    )(cond8, ada_w, ada_b.reshape(depth, 1, n6))


def _proj_kernel(x_ref, mod_ref, g_ref, w_ref, o_ref, h_scr, *, n_ctx, tm):
    row0 = pl.program_id(1) * tm

    @pl.when(pl.program_id(2) == 0)
    def _():
        _norm_mod_into(h_scr, x_ref[0], g_ref[...], mod_ref, row0, n_ctx, 0, 1)

    o_ref[0] = jnp.dot(h_scr[...], w_ref[...], preferred_element_type=F32).astype(o_ref.dtype)


def _proj(xc, mods, g, w, n_ctx):
    b, n, d = xc.shape
    nout = w.shape[1]
    tm = _pick_tile(n, 640, 128)
    tn = _pick_tile(nout, 1024, V7X_LANES)
    return pl.pallas_call(
        functools.partial(_proj_kernel, n_ctx=n_ctx, tm=tm),
        grid=(b, n // tm, nout // tn),
        in_specs=[pl.BlockSpec((1, tm, d), lambda bi, i, j: (bi, i, 0)),
                  pl.BlockSpec((1, 2, 6, d), lambda bi, i, j: (bi, 0, 0, 0)),
                  pl.BlockSpec((1, d), lambda bi, i, j: (0, 0)),
                  pl.BlockSpec((d, tn), lambda bi, i, j: (0, j))],
        out_specs=pl.BlockSpec((1, tm, tn), lambda bi, i, j: (bi, i, j)),
        out_shape=jax.ShapeDtypeStruct((b, n, nout), BF16),
        scratch_shapes=[pltpu.VMEM((tm, d), BF16)],
        compiler_params=_cparams(("parallel", "parallel", "arbitrary")),
        name="proj",
    )(xc, mods, g.reshape(1, d), w)


def _qk_prep_kernel(q_ref, k_ref, cos_ref, sin_ref, gq_ref, gk_ref, grp_ref, qo_ref, ko_ref, *, heads, q_scale):
    cos = cos_ref[...]
    sin = sin_ref[...]
    w2 = 2 * V_DIM
    for src, g_ref, dst, sc in ((q_ref, gq_ref, qo_ref, q_scale), (k_ref, gk_ref, ko_ref, 1.0)):
        g = jnp.concatenate([g_ref[...], g_ref[...]], axis=1)
        for hp in range(heads // 2):
            x = src[0, :, hp * w2:(hp + 1) * w2].astype(F32)
            t = x * x
            t_hi = t.astype(BF16)
            t_lo = (t - t_hi.astype(F32)).astype(BF16)
            ss = (jnp.dot(t_hi, grp_ref[...], preferred_element_type=F32)
                  + jnp.dot(t_lo, grp_ref[...], preferred_element_type=F32))
            y = x * lax.rsqrt(ss * (1.0 / HEAD_DIM) + EPS) * g
            for hh in range(2):
                yh = y[:, hh * V_DIM:(hh + 1) * V_DIM]
                yh = yh * cos + pltpu.roll(yh, 64, axis=1) * sin
                c0 = hp * w2 + hh * V_DIM
                dst[0, :, c0:c0 + V_DIM] = (yh * sc).astype(BF16)


def _qk_prep(u, cos_t, sin_t, gq, gk, heads, q_col, k_col):
    b, n, _ = u.shape
    dq = heads * V_DIM
    assert heads % 2 == 0
    tm = _pick_tile(n, 640, 128)
    out = jax.ShapeDtypeStruct((b, n, dq), BF16)
    lane = np.arange(2 * V_DIM)
    group = ((lane[:, None] // V_DIM == lane[None, :] // V_DIM)
             & (lane[:, None] % 2 == lane[None, :] % 2)).astype(np.float32)
    return pl.pallas_call(
        functools.partial(_qk_prep_kernel, heads=heads, q_scale=Q_SCALE),
        grid=(b, n // tm),
        in_specs=[pl.BlockSpec((1, tm, dq), lambda bi, i: (bi, i, q_col)),
                  pl.BlockSpec((1, tm, dq), lambda bi, i: (bi, i, k_col)),
                  pl.BlockSpec((tm, V_DIM), lambda bi, i: (i, 0)),
                  pl.BlockSpec((tm, V_DIM), lambda bi, i: (i, 0)),
                  pl.BlockSpec((1, V_DIM), lambda bi, i: (0, 0)),
                  pl.BlockSpec((1, V_DIM), lambda bi, i: (0, 0)),
                  pl.BlockSpec((2 * V_DIM, 2 * V_DIM), lambda bi, i: (0, 0))],
        out_specs=[pl.BlockSpec((1, tm, dq), lambda bi, i: (bi, i, 0)),
                   pl.BlockSpec((1, tm, dq), lambda bi, i: (bi, i, 0))],
        out_shape=[out, out],
        compiler_params=_cparams(("parallel", "parallel")),
        name="qk_prep",
    )(u, u, cos_t, sin_t, gq, gk, jnp.asarray(group, BF16))


def _attn_kernel(lamv_ref, hg_ref, q_ref, k_ref, vt_ref, o_ref, m_scr, l_scr, acc_scr, cm_scr, alpha_scr,
                 s_scr, p_scr,
                 *, n_ctx, tk, n_lat_chunks, lam_init, rb):
    q = q_ref[0]
    tq = q.shape[0]
    lane = lax.broadcasted_iota(jnp.int32, q.shape, 1)
    even = (lane % 2) == 0
    zero = jnp.zeros_like(q)
    qm = (jnp.where(even, q, zero), jnp.where(even, zero, q))

    m_scr[...] = jnp.full(m_scr.shape, NEG_BIG, F32)
    l_scr[...] = jnp.zeros(l_scr.shape, F32)
    acc_scr[...] = jnp.zeros(acc_scr.shape, F32)

    def scores(kc, sb):
        rows = kc.shape[0]
        for mp in range(2):
            s = lax.dot_general(kc, qm[mp], (((1,), (1,)), ((), ())), preferred_element_type=F32)
            s_scr[sb, mp, 0:rows, :] = s
            cm_scr[sb, mp] = jnp.broadcast_to(jnp.max(s, axis=0, keepdims=True), (8, tq))

    def softmax(sb, rows):
        for mp in range(2):
            m_old = m_scr[mp]
            m_new = jnp.maximum(m_old, cm_scr[sb, mp])
            m_scr[mp] = m_new
            alpha = jnp.exp2(m_old - m_new)
            alpha_scr[sb, mp] = alpha
            lsum = jnp.zeros((8, tq), F32)
            for r0 in range(0, rows, rb):
                p = jnp.exp2(s_scr[sb, mp, r0:r0 + rb, :] - m_new[0:1, :])
                lsum = lsum + jnp.sum(p.reshape(rb // 8, 8, tq), axis=0)
                p_scr[sb, mp, r0:r0 + rb, :] = p.astype(BF16)
            l_scr[mp] = alpha * l_scr[mp] + jnp.broadcast_to(jnp.sum(lsum, axis=0, keepdims=True), (8, tq))

    def values(vtc, sb):
        rows = vtc.shape[1]
        for mp in range(2):
            acc_scr[mp] = alpha_scr[sb, mp, 0:1, :] * acc_scr[mp] + jnp.dot(
                vtc, p_scr[sb, mp, 0:rows, :], preferred_element_type=F32)

    units = tk // n_ctx

    def k_chunk(j):
        return k_ref[0, pl.ds(pl.multiple_of(n_ctx + (j - 1) * tk, n_ctx), tk), :]

    def vt_chunk(j):
        return jnp.concatenate([vt_ref[0, 0, 1 + (j - 1) * units + u] for u in range(units)], axis=1)

    n_chunks = n_lat_chunks + 1
    scores(k_ref[0, 0:n_ctx, :], 0)
    if n_chunks == 1:
        softmax(0, n_ctx)
        values(vt_ref[0, 0, 0], 0)
    else:
        assert n_chunks % 2 == 1
        scores(k_chunk(1), 1)
        softmax(0, n_ctx)
        scores(k_chunk(2), 0)
        softmax(1, tk)
        values(vt_ref[0, 0, 0], 0)

        def two_steps(i, carry):
            t = 2 * i + 1
            softmax(0, tk)
            scores(k_chunk(t + 2), 1)
            values(vt_chunk(t), 1)
            softmax(1, tk)
            scores(k_chunk(t + 3), 0)
            values(vt_chunk(t + 1), 0)
            return carry

        lax.fori_loop(0, (n_chunks - 3) // 2, two_steps, 0)
        softmax(0, tk)
        values(vt_chunk(n_chunks - 2), 1)
        values(vt_chunk(n_chunks - 1), 0)

    lv = lamv_ref[...]
    lam = (jnp.exp(jnp.sum(lv[0:1] * lv[1:2], axis=1, keepdims=True))
           - jnp.exp(jnp.sum(lv[2:3] * lv[3:4], axis=1, keepdims=True)) + lam_init)
    ot = acc_scr[0] / l_scr[0, 0:1, :] - lam * (acc_scr[1] / l_scr[1, 0:1, :])
    o = ot.T
    o = o * lax.rsqrt(jnp.mean(o * o, axis=-1, keepdims=True) + EPS)
    o_ref[0] = (o * hg_ref[...] * (1.0 - lam_init)).astype(o_ref.dtype)


def _attn_fast_kernel(lamv_ref, hg_ref, q_ref, k_ref, vt_ref, o_ref, l_scr, acc_scr, p_scr,
                      *, n_ctx, tk, n_lat_chunks, lam_init, rb, unroll):
    q = q_ref[0]
    tq = q.shape[0]
    lane = lax.broadcasted_iota(jnp.int32, q.shape, 1)
    even = (lane % 2) == 0
    zero = jnp.zeros_like(q)
    qm = (jnp.where(even, q, zero), jnp.where(even, zero, q))
    nt = (((1,), (1,)), ((), ()))
    ref = lamv_ref[4:5, 0:1]

    l_scr[...] = jnp.zeros(l_scr.shape, F32)
    acc_scr[...] = jnp.zeros(acc_scr.shape, F32)

    def scores_softmax(kc, pb):
        rows = kc.shape[0]
        for mp in range(2):
            lsum = jnp.zeros((8, tq), F32)
            for r0 in range(0, rows, rb):
                s = lax.dot_general(kc[r0:r0 + rb, :], qm[mp], nt, preferred_element_type=F32)
                p = jnp.exp2(s - ref)
                lsum = lsum + jnp.sum(p.reshape(rb // 8, 8, tq), axis=0)
                p_scr[pb, mp, r0:r0 + rb, :] = p.astype(BF16)
            l_scr[mp] += lsum

    def values(vtc, pb):
        rows = vtc.shape[1]
        for mp in range(2):
            acc_scr[mp] += jnp.dot(vtc, p_scr[pb, mp, 0:rows, :], preferred_element_type=F32)

    units = tk // n_ctx

    def k_chunk(j):
        return k_ref[0, pl.ds(pl.multiple_of(n_ctx + (j - 1) * tk, n_ctx), tk), :]

    def vt_chunk(j):
        return jnp.concatenate([vt_ref[0, 0, 1 + (j - 1) * units + u] for u in range(units)], axis=1)

    scores_softmax(k_ref[0, 0:n_ctx, :], 0)
    if n_lat_chunks == 0:
        values(vt_ref[0, 0, 0], 0)
    else:
        assert n_lat_chunks % 2 == 0
        scores_softmax(k_chunk(1), 1)
        values(vt_ref[0, 0, 0], 0)

        def two_steps(t):
            scores_softmax(k_chunk(t), 0)
            values(vt_chunk(t - 1), 1)
            scores_softmax(k_chunk(t + 1), 1)
            values(vt_chunk(t), 0)

        n_pairs = (n_lat_chunks - 2) // 2
        n_trips = n_pairs // unroll

        def trip(i, carry):
            for u in range(unroll):
                two_steps(2 + 2 * (i * unroll + u))
            return carry

        if n_trips:
            lax.fori_loop(0, n_trips, trip, 0)
        for pr in range(n_trips * unroll, n_pairs):
            two_steps(2 + 2 * pr)
        scores_softmax(k_chunk(n_lat_chunks), 0)
        values(vt_chunk(n_lat_chunks - 1), 1)
        values(vt_chunk(n_lat_chunks), 0)

    lv = lamv_ref[...]
    lam = (jnp.exp(jnp.sum(lv[0:1] * lv[1:2], axis=1, keepdims=True))
           - jnp.exp(jnp.sum(lv[2:3] * lv[3:4], axis=1, keepdims=True)) + lam_init)
    l0 = jnp.sum(l_scr[0], axis=0, keepdims=True)
    l1 = jnp.sum(l_scr[1], axis=0, keepdims=True)
    ot = acc_scr[0] / l0 - lam * (acc_scr[1] / l1)
    o = ot.T
    o = o * lax.rsqrt(jnp.mean(o * o, axis=-1, keepdims=True) + EPS)
    o_ref[0] = (o * hg_ref[...] * (1.0 - lam_init)).astype(o_ref.dtype)


def _with_prev(kernel_fn, n_in):
    def wrapped(*refs, **kw):
        kernel_fn(*refs[:n_in], *refs[n_in + 1:], **kw)
    return wrapped


def _attention_calls(kernel_fn, lead, lead_specs, scratch_fn, q, k, vt, n_ctx, kw, name):
    b, n, dq = q.shape
    heads = dq // V_DIM
    tq = n_ctx
    s = n - n_ctx
    n_in = len(lead) + 3
    o = pl.pallas_call(
        functools.partial(kernel_fn, n_lat_chunks=s // kw["tk"], **kw),
        grid=(b, heads, s // tq),
        in_specs=lead_specs + [
            pl.BlockSpec((1, tq, V_DIM), lambda bi, h, i: (bi, i + 1, h)),
            pl.BlockSpec((1, n, V_DIM), lambda bi, h, i: (bi, 0, h)),
            pl.BlockSpec((1, 1, n // n_ctx, V_DIM, n_ctx), lambda bi, h, i: (bi, h, 0, 0, 0))],
        out_specs=pl.BlockSpec((1, tq, V_DIM), lambda bi, h, i: (bi, i + 1, h)),
        out_shape=jax.ShapeDtypeStruct((b, n, dq), BF16),
        scratch_shapes=scratch_fn(tq),
        compiler_params=_cparams(("parallel", "parallel", "arbitrary")),
        name=name,
    )(*lead, q, k, vt)
    return pl.pallas_call(
        functools.partial(_with_prev(kernel_fn, n_in), n_lat_chunks=0, **kw),
        grid=(b, heads, 1),
        in_specs=lead_specs + [
            pl.BlockSpec((1, tq, V_DIM), lambda bi, h, i: (bi, 0, h)),
            pl.BlockSpec((1, n_ctx, V_DIM), lambda bi, h, i: (bi, 0, h)),
            pl.BlockSpec((1, 1, 1, V_DIM, n_ctx), lambda bi, h, i: (bi, h, 0, 0, 0)),
            pl.BlockSpec(memory_space=pl.ANY)],
        out_specs=pl.BlockSpec((1, tq, V_DIM), lambda bi, h, i: (bi, 0, h)),
        out_shape=jax.ShapeDtypeStruct((b, n, dq), BF16),
        scratch_shapes=scratch_fn(tq),
        input_output_aliases={n_in: 0},
        compiler_params=_cparams(("parallel", "parallel", "arbitrary")),
        name=name + "_ctx",
    )(*lead, q, k, vt, o)


def _attention(lam_vecs, head_g, q, k, vt, n_ctx, lam_init, bound):
    b, n, dq = q.shape
    s = n - n_ctx
    tk = _pick_tile(s // 2, 1024, n_ctx)
    rb = _pick_tile(n_ctx, 128, 8)
    kw = dict(n_ctx=n_ctx, tk=tk, lam_init=lam_init, rb=rb)
    lamv = jnp.concatenate([lam_vecs, jnp.full((4, HEAD_DIM), bound, F32)], axis=0)
    spec2 = [pl.BlockSpec((8, HEAD_DIM), lambda bi, h, i: (0, 0)),
             pl.BlockSpec((1, V_DIM), lambda bi, h, i: (0, 0))]

    def robust_scratch(tq):
        return [pltpu.VMEM((2, 8, tq), F32), pltpu.VMEM((2, 8, tq), F32),
                pltpu.VMEM((2, V_DIM, tq), F32), pltpu.VMEM((2, 2, 8, tq), F32),
                pltpu.VMEM((2, 2, 8, tq), F32), pltpu.VMEM((2, 2, tk, tq), F32),
                pltpu.VMEM((2, 2, tk, tq), BF16)]

    def fast_scratch(tq):
        return [pltpu.VMEM((2, 8, tq), F32), pltpu.VMEM((2, V_DIM, tq), F32),
                pltpu.VMEM((2, 2, tk, tq), BF16)]

    def fast(args):
        return _attention_calls(_attn_fast_kernel, [lamv, head_g], spec2, fast_scratch,
                                *args, n_ctx, dict(kw, unroll=FAST_ATTN_UNROLL), "diff_attn")

    def robust(args):
        return _attention_calls(_attn_kernel, [lamv, head_g], spec2, robust_scratch,
                                *args, n_ctx, kw, "diff_attn_online")

    return lax.cond(bound <= SAFE_SCORE_BOUND, fast, robust, (q, k, vt))


def _dft_cos_sin(n, rows=None, cols=None, mod=None):
    mod = n if mod is None else mod
    r = jnp.arange(n if rows is None else rows, dtype=jnp.int32)[:, None]
    c = jnp.arange(n if cols is None else cols, dtype=jnp.int32)[None, :]
    ang = ((r * c) % mod).astype(F32) * (2.0 * math.pi / mod)
    return jnp.cos(ang), jnp.sin(ang)


def _channel_dft(x, cs_ref, groups):
    cg = x.shape[1] // groups
    wr, wi = [], []
    for g in range(groups):
        w = jnp.dot(x[:, g * cg:(g + 1) * cg], cs_ref[...], preferred_element_type=F32)
        wr.append(w[:, :cg])
        wi.append(w[:, cg:])
    return jnp.concatenate(wr, axis=1), jnp.concatenate(wi, axis=1)


def _fourier1_kernel(x_ref, cs_ref, kc_ref, ks_ref, tc_ref, ts_ref, zr_ref, zi_ref, *, groups):
    ls, tf, c = x_ref.shape[1:]
    x = x_ref[0].reshape(ls * tf, c)
    wr, wi = _channel_dft(x, cs_ref, groups)
    wr = wr.astype(BF16)
    wi = wi.astype(BF16)
    kc = kc_ref[...]
    ks = ks_ref[...]
    zr = jnp.dot(kc, wr, preferred_element_type=F32) + jnp.dot(ks, wi, preferred_element_type=F32)
    zi = jnp.dot(kc, wi, preferred_element_type=F32) - jnp.dot(ks, wr, preferred_element_type=F32)
    reps = c // V7X_LANES
    tc = jnp.concatenate([tc_ref[0]] * reps, axis=1)
    ts = jnp.concatenate([ts_ref[0]] * reps, axis=1)
    zr_ref[0] = (zr * tc + zi * ts).astype(BF16).reshape(ls, tf, c)
    zi_ref[0] = (zi * tc - zr * ts).astype(BF16).reshape(ls, tf, c)


def _fourier2_kernel(zr_ref, zi_ref, fc_ref, fs_ref, y_ref, *, scale):
    for u in range(zr_ref.shape[1]):
        y = (jnp.dot(fc_ref[...], zr_ref[0, u], preferred_element_type=F32)
             + jnp.dot(fs_ref[...], zi_ref[0, u], preferred_element_type=F32))
        y_ref[0, u] = (y * scale).astype(BF16)


def _fourier_ctx_kernel(x_ref, cs_ref, pc_ref, ps_ref, y_ref, *, groups, scale):
    wr, wi = _channel_dft(x_ref[0], cs_ref, groups)
    y = (jnp.dot(pc_ref[...], wr.astype(BF16), preferred_element_type=F32)
         + jnp.dot(ps_ref[...], wi.astype(BF16), preferred_element_type=F32))
    y_ref[0] = (y * scale).astype(BF16)


def _fourier_tables(s, c):
    cg = c // N_FOURIER_GROUPS
    cc, sc = _dft_cos_sin(cg)
    cs = jnp.concatenate([cc, -sc], axis=1).astype(BF16)
    return cg, cs


def _fourier_latent(fx, groups):
    b, s, c = fx.shape
    cg, cs = _fourier_tables(s, c)
    ls = FOURIER_SLOW
    lf = s // ls
    tf = V7X_BF16_SUBLANES
    assert ls * lf == s and lf % tf == 0
    rows = ls * tf
    pc, ps = _dft_cos_sin(ls)
    eye = jnp.eye(tf, dtype=F32)
    kc = jnp.kron(pc, eye).astype(BF16)
    ks = jnp.kron(ps, eye).astype(BF16)
    u_idx = jnp.repeat(jnp.arange(ls, dtype=jnp.int32), tf)[None, :]
    f_idx = (jnp.arange(lf // tf, dtype=jnp.int32)[:, None] * tf
             + jnp.tile(jnp.arange(tf, dtype=jnp.int32), ls)[None, :])
    ang = ((u_idx * f_idx) % s).astype(F32) * (2.0 * math.pi / s)
    tc = jnp.broadcast_to(jnp.cos(ang)[:, :, None], (lf // tf, rows, V7X_LANES))
    ts = jnp.broadcast_to(jnp.sin(ang)[:, :, None], (lf // tf, rows, V7X_LANES))

    x4 = fx.reshape(b, ls, lf, c)
    zshape = jax.ShapeDtypeStruct((b, ls, lf, c), BF16)
    blk = pl.BlockSpec((1, ls, tf, c), lambda bi, i: (bi, 0, i, 0))
    zr, zi = pl.pallas_call(
        functools.partial(_fourier1_kernel, groups=groups),
        grid=(b, lf // tf),
        in_specs=[blk,
                  pl.BlockSpec((cg, 2 * cg), lambda bi, i: (0, 0)),
                  pl.BlockSpec((rows, rows), lambda bi, i: (0, 0)),
                  pl.BlockSpec((rows, rows), lambda bi, i: (0, 0)),
                  pl.BlockSpec((1, rows, V7X_LANES), lambda bi, i: (i, 0, 0)),
                  pl.BlockSpec((1, rows, V7X_LANES), lambda bi, i: (i, 0, 0))],
        out_specs=[blk, blk],
        out_shape=[zshape, zshape],
        compiler_params=_cparams(("parallel", "parallel")),
        name="fourier_slow",
    )(x4, cs, kc, ks, tc, ts)

    fc, fs = _dft_cos_sin(lf)
    tu = _pick_tile(ls, 4, 1)
    blk2 = pl.BlockSpec((1, tu, lf, c), lambda bi, i: (bi, i, 0, 0))
    yt = pl.pallas_call(
        functools.partial(_fourier2_kernel, scale=1.0 / math.sqrt(s * cg)),
        grid=(b, ls // tu),
        in_specs=[blk2, blk2,
                  pl.BlockSpec((lf, lf), lambda bi, i: (0, 0)),
                  pl.BlockSpec((lf, lf), lambda bi, i: (0, 0))],
        out_specs=blk2,
        out_shape=zshape,
        compiler_params=_cparams(("parallel", "parallel")),
        name="fourier_fast",
    )(zr, zi, fc.astype(BF16), fs.astype(BF16))
    return jnp.swapaxes(yt, 1, 2).reshape(b, s, c)


def _fourier_ctx(fxc, groups):
    b, n_ctx, c = fxc.shape
    cg, cs = _fourier_tables(n_ctx, c)
    pc, ps = _dft_cos_sin(n_ctx)
    return pl.pallas_call(
        functools.partial(_fourier_ctx_kernel, groups=groups, scale=1.0 / math.sqrt(n_ctx * cg)),
        grid=(b,),
        in_specs=[pl.BlockSpec((1, n_ctx, c), lambda bi: (bi, 0, 0)),
                  pl.BlockSpec((cg, 2 * cg), lambda bi: (0, 0)),
                  pl.BlockSpec((n_ctx, n_ctx), lambda bi: (0, 0)),
                  pl.BlockSpec((n_ctx, n_ctx), lambda bi: (0, 0))],
        out_specs=pl.BlockSpec((1, n_ctx, c), lambda bi: (bi, 0, 0)),
        out_shape=jax.ShapeDtypeStruct((b, n_ctx, c), BF16),
        compiler_params=_cparams(("parallel",)),
        name="fourier_ctx",
    )(fxc, cs, pc.astype(BF16), ps.astype(BF16))


def _out_ab_kernel(x_ref, f_ref, o_ref_in, mod_ref, w_ref, out_ref, *, n_ctx, tm, c):
    (gate,) = _row_mods(mod_ref, pl.program_id(1) * tm, tm, n_ctx, (2,))
    y = (jnp.dot(f_ref[0], w_ref[0:c, :], preferred_element_type=F32)
         + jnp.dot(o_ref_in[0], w_ref[c:, :], preferred_element_type=F32))
    out_ref[0] = x_ref[0] + gate * y


def _out_ab(xc, f, o, mods, w_out, n_ctx):
    b, n, d = xc.shape
    c = f.shape[2]
    tm = _pick_tile(n, 640, 128)
    return pl.pallas_call(
        functools.partial(_out_ab_kernel, n_ctx=n_ctx, tm=tm, c=c),
        grid=(b, n // tm),
        in_specs=[pl.BlockSpec((1, tm, d), lambda bi, i: (bi, i, 0)),
                  pl.BlockSpec((1, tm, c), lambda bi, i: (bi, i, 0)),
                  pl.BlockSpec((1, tm, o.shape[2]), lambda bi, i: (bi, i, 0)),
                  pl.BlockSpec((1, 2, 6, d), lambda bi, i: (bi, 0, 0, 0)),
                  pl.BlockSpec(w_out.shape, lambda bi, i: (0, 0))],
        out_specs=pl.BlockSpec((1, tm, d), lambda bi, i: (bi, i, 0)),
        out_shape=jax.ShapeDtypeStruct((b, n, d), F32),
        compiler_params=_cparams(("parallel", "parallel")),
        name="out_proj_attn",
    )(xc, f, o, mods, w_out)


def _out_lru_kernel(x_ref, yf_ref, yr_ref, gate_ref, mod_ref, w_ref, out_ref, *, n_ctx, tm):
    (gate,) = _row_mods(mod_ref, pl.program_id(1) * tm, tm, n_ctx, (2,))
    z = (yf_ref[0] + yr_ref[0]) * _gelu_tanh(gate_ref[0].astype(F32))
    y = jnp.dot(z.astype(BF16), w_ref[...], preferred_element_type=F32)
    out_ref[0] = x_ref[0] + gate * y


def _out_lru(xc, yf, yr, u, mods, w_out, n_ctx):
    b, n, d = xc.shape
    tm = _pick_tile(n, 320, 64)
    row = pl.BlockSpec((1, tm, d), lambda bi, i: (bi, i, 0))
    return pl.pallas_call(
        functools.partial(_out_lru_kernel, n_ctx=n_ctx, tm=tm),
        grid=(b, n // tm),
        in_specs=[row, row, row, row,
                  pl.BlockSpec((1, 2, 6, d), lambda bi, i: (bi, 0, 0, 0)),
                  pl.BlockSpec(w_out.shape, lambda bi, i: (0, 0))],
        out_specs=row,
        out_shape=jax.ShapeDtypeStruct((b, n, d), F32),
        compiler_params=_cparams(("parallel", "parallel")),
        name="out_proj_lru",
    )(xc, yf, yr, u, mods, w_out)


def _conv_kernel(main_ref, prev_ref, next_ref, w_ref, b_ref, o_ref, ext_scr, *, n_blocks, halo):
    i = pl.program_id(1)
    tb = main_ref.shape[1]
    first = jnp.logical_or(i == 0, i == 1)
    last = jnp.logical_or(i == 0, i == n_blocks - 1)
    w = w_ref[...]
    bias = b_ref[...]
    zero = jnp.minimum(i, 0)
    for j in range(w.shape[1] // V7X_LANES):
        cols = slice(j * V7X_LANES, (j + 1) * V7X_LANES)
        ext_scr[j, 0:halo, :] = jnp.where(first, 0.0, prev_ref[0, :, cols].astype(F32))
        ext_scr[j, halo:halo + tb, :] = main_ref[0, :, cols].astype(F32)
        ext_scr[j, halo + tb:, :] = jnp.where(last, 0.0, next_ref[0, :, cols].astype(F32))
        rs = min(tb, 64)
        for r0 in range(0, tb, rs):
            acc = jnp.broadcast_to(bias[:, cols], (rs, V7X_LANES))
            for k in range(w.shape[0]):
                acc = acc + w[k:k + 1, cols] * ext_scr[j, pl.ds(halo + r0 + k - 2 + zero, rs), :]
            o_ref[0, r0:r0 + rs, cols] = acc


def _conv(u, conv_w, conv_b, n_ctx, col):
    b, n, d2 = u.shape
    d = d2 // 2
    tb = n_ctx
    halo = V7X_BF16_SUBLANES
    nb = n // tb
    hb = tb // halo
    return pl.pallas_call(
        functools.partial(_conv_kernel, n_blocks=nb, halo=halo),
        grid=(b, nb),
        in_specs=[pl.BlockSpec((1, tb, d), lambda bi, i: (bi, i, col)),
                  pl.BlockSpec((1, halo, d), lambda bi, i: (bi, jnp.maximum(i * hb - 1, 0), col)),
                  pl.BlockSpec((1, halo, d), lambda bi, i: (bi, jnp.minimum((i + 1) * hb, n // halo - 1), col)),
                  pl.BlockSpec(conv_w.shape, lambda bi, i: (0, 0)),
                  pl.BlockSpec((1, d), lambda bi, i: (0, 0))],
        out_specs=pl.BlockSpec((1, tb, d), lambda bi, i: (bi, i, 0)),
        out_shape=jax.ShapeDtypeStruct((b, n, d), F32),
        scratch_shapes=[pltpu.VMEM((d // V7X_LANES, tb + 2 * halo, V7X_LANES), F32)],
        compiler_params=_cparams(("parallel", "parallel")),
        name="lru_conv",
    )(u, u, u, conv_w, conv_b.reshape(1, d))


def _lru_scan_kernel(xf_ref, xr_ref, gaw_ref, gab_ref, gxw_ref, gxb_ref, lam_ref, yf_ref, yr_ref,
                     h_scr, a0_scr, a1_scr, b0_scr, b1_scr, y0_scr, y1_scr):
    t = pl.program_id(1)
    tb, d = xf_ref.shape[1:]
    nblk = gaw_ref.shape[1]
    bs = d // nblk
    nl = d // V7X_LANES
    pitch = SCAN_ROW_PITCH
    a_scr, b_scr, y_scr = (a0_scr, a1_scr), (b0_scr, b1_scr), (y0_scr, y1_scr)

    @pl.when(t == 0)
    def _():
        h_scr[...] = jnp.zeros(h_scr.shape, F32)

    def block_diag(xb, w_ref, dr):
        return jnp.concatenate(
            [jnp.dot(xb[:, n * bs:(n + 1) * bs], w_ref[dr, n], preferred_element_type=F32)
             for n in range(nblk)], axis=1)

    for dr, x_ref in enumerate((xf_ref, xr_ref)):
        xc = x_ref[0]
        xb = xc.astype(BF16)
        th_a = jnp.tanh(block_diag(xb, gaw_ref, dr) + gab_ref[dr:dr + 1, :])
        th_x = jnp.tanh(block_diag(xb, gxw_ref, dr) + gxb_ref[dr:dr + 1, :])
        z = -lam_ref[dr:dr + 1, :]
        softplus = jnp.maximum(z, 0.0) + jnp.log1p(jnp.exp(-jnp.abs(z)))
        c = (-0.5 * LRU_C) * softplus
        a = jnp.exp(c + c * th_a)
        w = 1.0 - a * a
        mult = jnp.where(w > 0.0, w * lax.rsqrt(w), 0.0)
        bb = mult * (0.5 * xc) * (1.0 + th_x)
        for j in range(nl):
            cols = slice(j * V7X_LANES, (j + 1) * V7X_LANES)
            a_scr[dr][pl.ds(j, tb, stride=pitch), :] = a[:, cols]
            b_scr[dr][pl.ds(j, tb, stride=pitch), :] = bb[:, cols]

    def step(i, carry):
        hf, hr = carry
        rf = i * pitch
        hf = a_scr[0][pl.ds(rf, nl), :] * hf + b_scr[0][pl.ds(rf, nl), :]
        y_scr[0][pl.ds(rf, nl), :] = hf
        rr = (tb - 1 - i) * pitch
        hr = a_scr[1][pl.ds(rr, nl), :] * hr + b_scr[1][pl.ds(rr, nl), :]
        y_scr[1][pl.ds(rr, nl), :] = hr
        return hf, hr

    hf, hr = lax.fori_loop(0, tb, step, (h_scr[0], h_scr[1]), unroll=8)
    h_scr[0] = hf
    h_scr[1] = hr
    for j in range(nl):
        cols = slice(j * V7X_LANES, (j + 1) * V7X_LANES)
        yf_ref[0, :, cols] = y_scr[0][pl.ds(j, tb, stride=pitch), :]
        yr_ref[0, :, cols] = y_scr[1][pl.ds(j, tb, stride=pitch), :]


def _lru_scan(xcv, ga_w, ga_b, gx_w, gx_b, lam, n_ctx):
    b, n, d = xcv.shape
    tb = n_ctx
    nt = n // tb
    fwd = pl.BlockSpec((1, tb, d), lambda bi, t: (bi, t, 0))
    rev = pl.BlockSpec((1, tb, d), lambda bi, t: (bi, jnp.where(t == 0, 0, nt - t), 0))
    wspec = pl.BlockSpec(ga_w.shape, lambda bi, t: (0, 0, 0, 0))
    vspec = pl.BlockSpec((2, d), lambda bi, t: (0, 0))
    out = jax.ShapeDtypeStruct((b, n, d), F32)
    return pl.pallas_call(
        _lru_scan_kernel,
        grid=(b, nt),
        in_specs=[fwd, rev, wspec, vspec, wspec, vspec, vspec],
        out_specs=[fwd, rev],
        out_shape=[out, out],
        scratch_shapes=[pltpu.VMEM((2, d // V7X_LANES, V7X_LANES), F32)]
        + [pltpu.VMEM((tb * SCAN_ROW_PITCH, V7X_LANES), F32)] * 6,
        compiler_params=_cparams(("parallel", "arbitrary")),
        name="lru_scan",
    )(xcv, xcv, ga_w, ga_b, gx_w, gx_b, lam)


def _ffn_kernel(x_ref, mod_ref, g_ref, wg_ref, wu_ref, wd_ref, o_ref, h_scr, *, n_ctx, tm):
    k = pl.program_id(2)
    row0 = pl.program_id(1) * tm

    @pl.when(k == 0)
    def _():
        _norm_mod_into(h_scr, x_ref[0], g_ref[...], mod_ref, row0, n_ctx, 3, 4)
        o_ref[0] = jnp.zeros(o_ref.shape[1:], F32)

    h = h_scr[...]
    act = _silu(jnp.dot(h, wg_ref[...], preferred_element_type=F32)) * jnp.dot(
        h, wu_ref[...], preferred_element_type=F32)
    o_ref[0] += jnp.dot(act.astype(BF16), wd_ref[...], preferred_element_type=F32)

    @pl.when(k == pl.num_programs(2) - 1)
    def _():
        (gate,) = _row_mods(mod_ref, row0, tm, n_ctx, (5,))
        o_ref[0] = x_ref[0] + gate * o_ref[0]


def _ffn(xc, mods, g, wg, wu, wd, n_ctx):
    b, n, d = xc.shape
    dff = wg.shape[1]
    tm = _pick_tile(n, 640, 128)
    tf = _pick_tile(dff, 512, V7X_LANES)
    return pl.pallas_call(
        functools.partial(_ffn_kernel, n_ctx=n_ctx, tm=tm),
        grid=(b, n // tm, dff // tf),
        in_specs=[pl.BlockSpec((1, tm, d), lambda bi, i, k: (bi, i, 0)),
                  pl.BlockSpec((1, 2, 6, d), lambda bi, i, k: (bi, 0, 0, 0)),
                  pl.BlockSpec((1, d), lambda bi, i, k: (0, 0)),
                  pl.BlockSpec((d, tf), lambda bi, i, k: (0, k)),
                  pl.BlockSpec((d, tf), lambda bi, i, k: (0, k)),
                  pl.BlockSpec((tf, d), lambda bi, i, k: (k, 0))],
        out_specs=pl.BlockSpec((1, tm, d), lambda bi, i, k: (bi, i, 0)),
        out_shape=jax.ShapeDtypeStruct((b, n, d), F32),
        scratch_shapes=[pltpu.VMEM((tm, d), BF16)],
        compiler_params=_cparams(("parallel", "parallel", "arbitrary")),
        name="ffn",
    )(xc, mods, g.reshape(1, d), wg, wu, wd)


def _permute_qk_cols(w, heads):
    d = w.shape[0]
    w5 = w.reshape(d, heads, 2, HEAD_DIM // 2, 2)
    return jnp.transpose(w5, (0, 1, 4, 3, 2)).reshape(d, heads * V_DIM)


def _qk_gain_index():
    gain_idx = np.zeros(V_DIM, np.int32)
    for half in range(2):
        for j in range(HEAD_DIM // 2):
            for mp in range(2):
                gain_idx[half * HEAD_DIM + 2 * j + mp] = 2 * j + half
    return gain_idx


def _rope_tables(n_ctx, s):
    n_rows = s // GRID_W
    rows = jnp.repeat(jnp.arange(n_rows, dtype=F32), GRID_W, total_repeat_length=s)
    cols = jnp.tile(jnp.arange(GRID_W, dtype=F32), n_rows)
    n_axis = HEAD_DIM // 4
    inv = ROPE_THETA ** (-jnp.arange(n_axis, dtype=F32) / n_axis)
    ang = jnp.concatenate([rows[:, None] * inv, cols[:, None] * inv], axis=-1)
    cos = jnp.concatenate([jnp.ones((n_ctx, HEAD_DIM // 2), F32), jnp.cos(ang)], axis=0)
    sin = jnp.concatenate([jnp.zeros((n_ctx, HEAD_DIM // 2), F32), jnp.sin(ang)], axis=0)
    cos_h = jnp.repeat(cos, 2, axis=1)
    sin_h = jnp.repeat(sin, 2, axis=1)
    return jnp.concatenate([cos_h, cos_h], axis=1), jnp.concatenate([-sin_h, sin_h], axis=1)


def kernel(x, c, ctx, c_ctx, ada_w, ada_b, norm_mix_g, norm_ffn_g, ffn_w_gate, ffn_w_up, ffn_w_down, ab_w_in, ab_w_out, ab_q_norm_g, ab_k_norm_g, ab_lam_q1, ab_lam_k1, ab_lam_q2, ab_lam_k2, ab_head_norm_g, lru_w_in, lru_w_out, lru_conv_w, lru_conv_b, lru_gate_a_w, lru_gate_a_b, lru_gate_x_w, lru_gate_x_b, lru_lambda):
    b, s, d = x.shape
    n_ctx = ctx.shape[1]
    depth = ada_w.shape[0]
    c_f = d // 2
    heads = (d // 2) // V_DIM
    dq = heads * V_DIM

    cond8 = jnp.zeros((8, d), F32).at[:b].set(c).at[b].set(c_ctx)
    ada = _adaln(cond8, ada_w, ada_b).reshape(depth, 8, 6, d)
    mods_all = jnp.stack([jnp.broadcast_to(ada[:, b][:, None], (depth, b, 6, d)), ada[:, :b]], axis=2)

    gain_idx = _qk_gain_index()
    cos_t, sin_t = _rope_tables(n_ctx, s)

    xc = jnp.concatenate([ctx, x], axis=1)

    for l in range(depth):
        j = l // 2
        mods = mods_all[l]
        if l % 2 == 0:
            lam_init = 0.8 - 0.6 * math.exp(-0.3 * l)
            w_in = ab_w_in[j]
            w_in = jnp.concatenate([w_in[:, :c_f],
                                    _permute_qk_cols(w_in[:, c_f:c_f + dq], heads),
                                    _permute_qk_cols(w_in[:, c_f + dq:c_f + 2 * dq], heads),
                                    w_in[:, c_f + 2 * dq:]], axis=1).astype(BF16)
            u = _proj(xc, mods, norm_mix_g[l], w_in, n_ctx)
            gq = jnp.take(ab_q_norm_g[j], gain_idx).reshape(1, V_DIM)
            gk = jnp.take(ab_k_norm_g[j], gain_idx).reshape(1, V_DIM)
            q, k = _qk_prep(u, cos_t, sin_t, gq, gk, heads, c_f // dq, c_f // dq + 1)
            v = u[:, :, c_f + 2 * dq:]
            vt = jnp.transpose(v.reshape(b, (n_ctx + s) // n_ctx, n_ctx, heads, V_DIM), (0, 3, 1, 4, 2))
            lamv = jnp.stack([ab_lam_q1[j], ab_lam_k1[j], ab_lam_q2[j], ab_lam_k2[j]])
            bound = (HEAD_DIM * Q_SCALE * BOUND_MARGIN
                     * jnp.max(jnp.abs(ab_q_norm_g[j])) * jnp.max(jnp.abs(ab_k_norm_g[j])))
            o = _attention(lamv, ab_head_norm_g[j].reshape(1, V_DIM), q, k, vt, n_ctx, lam_init, bound)
            f_ctx = _fourier_ctx(u[:, :n_ctx, :c_f], N_FOURIER_GROUPS)
            f_lat = _fourier_latent(u[:, n_ctx:, :c_f], N_FOURIER_GROUPS)
            f = jnp.concatenate([f_ctx, f_lat], axis=1)
            xc = _out_ab(xc, f, o, mods, ab_w_out[j].astype(BF16), n_ctx)
        else:
            u = _proj(xc, mods, norm_mix_g[l], lru_w_in[j].astype(BF16), n_ctx)
            xcv = _conv(u, lru_conv_w[j], lru_conv_b[j], n_ctx, 1)
            yf, yr = _lru_scan(xcv, (0.5 * lru_gate_a_w[j]).astype(BF16), 0.5 * lru_gate_a_b[j],
                               (0.5 * lru_gate_x_w[j]).astype(BF16), 0.5 * lru_gate_x_b[j],
                               lru_lambda[j], n_ctx)
            xc = _out_lru(xc, yf, yr, u, mods, lru_w_out[j].astype(BF16), n_ctx)
        xc = _ffn(xc, mods, norm_ffn_g[l], ffn_w_gate[l].astype(BF16), ffn_w_up[l].astype(BF16),
                  ffn_w_down[l].astype(BF16), n_ctx)
    return xc[:, n_ctx:]
```

```python
import functools
import math

import jax
import jax.numpy as jnp
import numpy as np
from jax import lax
from jax.experimental import pallas as pl
from jax.experimental.pallas import tpu as pltpu

F32 = jnp.float32
BF16 = jnp.bfloat16

EPS = 1e-6
GRID_W = 64
ROPE_THETA = 10000.0
HEAD_DIM = 64
V_DIM = 2 * HEAD_DIM
N_FOURIER_GROUPS = 4
LRU_C = 8.0
LOG2E = 1.4426950408889634
Q_SCALE = (HEAD_DIM ** -0.5) * LOG2E

V7X_LANES = 128
V7X_BF16_SUBLANES = 16
V7X_VMEM_BYTES = 64 * 1024 * 1024
VMEM_LIMIT = V7X_VMEM_BYTES - 8 * 1024 * 1024

FOURIER_SLOW = 32
FAST_ATTN_UNROLL = 3
SCAN_ROW_PITCH = 20
NEG_BIG = -1e30
BOUND_MARGIN = 1.02
SAFE_SCORE_BOUND = 50.0


def _cparams(sem, flags=None):
    return pltpu.CompilerParams(dimension_semantics=sem, vmem_limit_bytes=VMEM_LIMIT, flags=flags)


def _pick_tile(n, target, mult):
    best = None
    for t in range(mult, min(n, target) + 1, mult):
        if n % t == 0:
            best = t
    assert best is not None, (n, target, mult)
    return best


def _row_mods(mod_ref, row0, rows, n_ctx, idx):
    r = row0 + lax.broadcasted_iota(jnp.int32, (rows, 1), 0)
    is_ctx = r < n_ctx
    return [jnp.where(is_ctx, mod_ref[0, 0, k:k + 1, :], mod_ref[0, 1, k:k + 1, :]) for k in idx]


def _norm_mod_into(h_scr, x, g, mod_ref, row0, n_ctx, i_shift, i_scale):
    y = x * lax.rsqrt(jnp.mean(x * x, axis=-1, keepdims=True) + EPS)

    @pl.when(row0 >= n_ctx)
    def _():
        a = g * (1.0 + mod_ref[0, 1, i_scale:i_scale + 1, :])
        h_scr[...] = (y * a + mod_ref[0, 1, i_shift:i_shift + 1, :]).astype(BF16)

    @pl.when(row0 < n_ctx)
    def _():
        shift, scale = _row_mods(mod_ref, row0, x.shape[0], n_ctx, (i_shift, i_scale))
        h_scr[...] = ((y * g) * (1.0 + scale) + shift).astype(BF16)


def _silu(x):
    return x * jax.nn.sigmoid(x)


def _gelu_tanh(x):
    return 0.5 * x * (1.0 + jnp.tanh(math.sqrt(2.0 / math.pi) * (x + 0.044715 * (x * x * x))))


def _adaln_kernel(cond_ref, w_ref, b_ref, o_ref):
    s = _silu(cond_ref[...])
    o_ref[0] = jnp.dot(s.astype(BF16), w_ref[0].astype(BF16),
                       preferred_element_type=F32) + b_ref[0]


def _adaln(cond8, ada_w, ada_b):
    depth, d, n6 = ada_w.shape
    tn = _pick_tile(n6, 1024, V7X_LANES)
    return pl.pallas_call(
        _adaln_kernel,
        grid=(depth, n6 // tn),
        in_specs=[pl.BlockSpec((8, d), lambda l, j: (0, 0)),
                  pl.BlockSpec((1, d, tn), lambda l, j: (l, 0, j)),
                  pl.BlockSpec((1, 1, tn), lambda l, j: (l, 0, j))],
        out_specs=pl.BlockSpec((1, 8, tn), lambda l, j: (l, 0, j)),
        out_shape=jax.ShapeDtypeStruct((depth, 8, n6), F32),
        compiler_params=_cparams(("parallel", "parallel")),
        name="adaln",
    )(cond8, ada_w, ada_b.reshape(depth, 1, n6))


def _proj_kernel(x_ref, mod_ref, g_ref, w_ref, o_ref, h_scr, *, n_ctx, tm):
    row0 = pl.program_id(1) * tm

    @pl.when(pl.program_id(2) == 0)
    def _():
        _norm_mod_into(h_scr, x_ref[0], g_ref[...], mod_ref, row0, n_ctx, 0, 1)

    o_ref[0] = jnp.dot(h_scr[...], w_ref[...], preferred_element_type=F32).astype(o_ref.dtype)


def _proj(xc, mods, g, w, n_ctx):
    b, n, d = xc.shape
    nout = w.shape[1]
    tm = _pick_tile(n, 1280, 128)
    tn = _pick_tile(nout, 1024, V7X_LANES)
    return pl.pallas_call(
        functools.partial(_proj_kernel, n_ctx=n_ctx, tm=tm),
        grid=(b, n // tm, nout // tn),
        in_specs=[pl.BlockSpec((1, tm, d), lambda bi, i, j: (bi, i, 0)),
                  pl.BlockSpec((1, 2, 6, d), lambda bi, i, j: (bi, 0, 0, 0)),
                  pl.BlockSpec((1, d), lambda bi, i, j: (0, 0)),
                  pl.BlockSpec((d, tn), lambda bi, i, j: (0, j))],
        out_specs=pl.BlockSpec((1, tm, tn), lambda bi, i, j: (bi, i, j)),
        out_shape=jax.ShapeDtypeStruct((b, n, nout), BF16),
        scratch_shapes=[pltpu.VMEM((tm, d), BF16)],
        compiler_params=_cparams(("parallel", "parallel", "arbitrary")),
        name="proj",
    )(xc, mods, g.reshape(1, d), w)


def _qk_prep_kernel(q_ref, k_ref, cos_ref, sin_ref, gq_ref, gk_ref, grp_ref, qo_ref, ko_ref, *, heads, q_scale):
    cos = cos_ref[...]
    sin = sin_ref[...]
    w2 = 2 * V_DIM
    for src, g_ref, dst, sc in ((q_ref, gq_ref, qo_ref, q_scale), (k_ref, gk_ref, ko_ref, 1.0)):
        g = jnp.concatenate([g_ref[...], g_ref[...]], axis=1)
        for hp in range(heads // 2):
            x = src[0, :, hp * w2:(hp + 1) * w2].astype(F32)
            t = x * x
            t_hi = t.astype(BF16)
            t_lo = (t - t_hi.astype(F32)).astype(BF16)
            ss = (jnp.dot(t_hi, grp_ref[...], preferred_element_type=F32)
                  + jnp.dot(t_lo, grp_ref[...], preferred_element_type=F32))
            y = x * lax.rsqrt(ss * (1.0 / HEAD_DIM) + EPS) * g
            for hh in range(2):
                yh = y[:, hh * V_DIM:(hh + 1) * V_DIM]
                yh = yh * cos + pltpu.roll(yh, 64, axis=1) * sin
                c0 = hp * w2 + hh * V_DIM
                dst[0, :, c0:c0 + V_DIM] = (yh * sc).astype(BF16)


def _qk_prep(u, cos_t, sin_t, gq, gk, heads, q_col, k_col):
    b, n, _ = u.shape
    dq = heads * V_DIM
    assert heads % 2 == 0
    tm = _pick_tile(n, 640, 128)
    out = jax.ShapeDtypeStruct((b, n, dq), BF16)
    lane = np.arange(2 * V_DIM)
    group = ((lane[:, None] // V_DIM == lane[None, :] // V_DIM)
             & (lane[:, None] % 2 == lane[None, :] % 2)).astype(np.float32)
    return pl.pallas_call(
        functools.partial(_qk_prep_kernel, heads=heads, q_scale=Q_SCALE),
        grid=(b, n // tm),
        in_specs=[pl.BlockSpec((1, tm, dq), lambda bi, i: (bi, i, q_col)),
                  pl.BlockSpec((1, tm, dq), lambda bi, i: (bi, i, k_col)),
                  pl.BlockSpec((tm, V_DIM), lambda bi, i: (i, 0)),
                  pl.BlockSpec((tm, V_DIM), lambda bi, i: (i, 0)),
                  pl.BlockSpec((1, V_DIM), lambda bi, i: (0, 0)),
                  pl.BlockSpec((1, V_DIM), lambda bi, i: (0, 0)),
                  pl.BlockSpec((2 * V_DIM, 2 * V_DIM), lambda bi, i: (0, 0))],
        out_specs=[pl.BlockSpec((1, tm, dq), lambda bi, i: (bi, i, 0)),
                   pl.BlockSpec((1, tm, dq), lambda bi, i: (bi, i, 0))],
        out_shape=[out, out],
        compiler_params=_cparams(("parallel", "parallel")),
        name="qk_prep",
    )(u, u, cos_t, sin_t, gq, gk, jnp.asarray(group, BF16))


def _attn_kernel(lamv_ref, hg_ref, q_ref, k_ref, vt_ref, o_ref, m_scr, l_scr, acc_scr, cm_scr, alpha_scr,
                 s_scr, p_scr,
                 *, n_ctx, tk, n_lat_chunks, lam_init, rb):
    q = q_ref[0]
    tq = q.shape[0]
    lane = lax.broadcasted_iota(jnp.int32, q.shape, 1)
    even = (lane % 2) == 0
    zero = jnp.zeros_like(q)
    qm = (jnp.where(even, q, zero), jnp.where(even, zero, q))

    m_scr[...] = jnp.full(m_scr.shape, NEG_BIG, F32)
    l_scr[...] = jnp.zeros(l_scr.shape, F32)
    acc_scr[...] = jnp.zeros(acc_scr.shape, F32)

    def scores(kc, sb):
        rows = kc.shape[0]
        for mp in range(2):
            s = lax.dot_general(kc, qm[mp], (((1,), (1,)), ((), ())), preferred_element_type=F32)
            s_scr[sb, mp, 0:rows, :] = s
            cm_scr[sb, mp] = jnp.broadcast_to(jnp.max(s, axis=0, keepdims=True), (8, tq))

    def softmax(sb, rows):
        for mp in range(2):
            m_old = m_scr[mp]
            m_new = jnp.maximum(m_old, cm_scr[sb, mp])
            m_scr[mp] = m_new
            alpha = jnp.exp2(m_old - m_new)
            alpha_scr[sb, mp] = alpha
            lsum = jnp.zeros((8, tq), F32)
            for r0 in range(0, rows, rb):
                p = jnp.exp2(s_scr[sb, mp, r0:r0 + rb, :] - m_new[0:1, :])
                lsum = lsum + jnp.sum(p.reshape(rb // 8, 8, tq), axis=0)
                p_scr[sb, mp, r0:r0 + rb, :] = p.astype(BF16)
            l_scr[mp] = alpha * l_scr[mp] + jnp.broadcast_to(jnp.sum(lsum, axis=0, keepdims=True), (8, tq))

    def values(vtc, sb):
        rows = vtc.shape[1]
        for mp in range(2):
            acc_scr[mp] = alpha_scr[sb, mp, 0:1, :] * acc_scr[mp] + jnp.dot(
                vtc, p_scr[sb, mp, 0:rows, :], preferred_element_type=F32)

    units = tk // n_ctx

    def k_chunk(j):
        return k_ref[0, pl.ds(pl.multiple_of(n_ctx + (j - 1) * tk, n_ctx), tk), :]

    def vt_chunk(j):
        return jnp.concatenate([vt_ref[0, 0, 1 + (j - 1) * units + u] for u in range(units)], axis=1)

    n_chunks = n_lat_chunks + 1
    scores(k_ref[0, 0:n_ctx, :], 0)
    if n_chunks == 1:
        softmax(0, n_ctx)
        values(vt_ref[0, 0, 0], 0)
    else:
        assert n_chunks % 2 == 1
        scores(k_chunk(1), 1)
        softmax(0, n_ctx)
        scores(k_chunk(2), 0)
        softmax(1, tk)
        values(vt_ref[0, 0, 0], 0)

        def two_steps(i, carry):
            t = 2 * i + 1
            softmax(0, tk)
            scores(k_chunk(t + 2), 1)
            values(vt_chunk(t), 1)
            softmax(1, tk)
            scores(k_chunk(t + 3), 0)
            values(vt_chunk(t + 1), 0)
            return carry

        lax.fori_loop(0, (n_chunks - 3) // 2, two_steps, 0)
        softmax(0, tk)
        values(vt_chunk(n_chunks - 2), 1)
        values(vt_chunk(n_chunks - 1), 0)

    lv = lamv_ref[...]
    lam = (jnp.exp(jnp.sum(lv[0:1] * lv[1:2], axis=1, keepdims=True))
           - jnp.exp(jnp.sum(lv[2:3] * lv[3:4], axis=1, keepdims=True)) + lam_init)
    ot = acc_scr[0] / l_scr[0, 0:1, :] - lam * (acc_scr[1] / l_scr[1, 0:1, :])
    o = ot.T
    o = o * lax.rsqrt(jnp.mean(o * o, axis=-1, keepdims=True) + EPS)
    o_ref[0] = (o * hg_ref[...] * (1.0 - lam_init)).astype(o_ref.dtype)


def _attn_fast_kernel(lamv_ref, hg_ref, q_ref, k_ref, vt_ref, o_ref, l_scr, acc_scr, p_scr,
                      *, n_ctx, tk, n_lat_chunks, lam_init, rb, unroll):
    q = q_ref[0]
    tq = q.shape[0]
    lane = lax.broadcasted_iota(jnp.int32, q.shape, 1)
    even = (lane % 2) == 0
    zero = jnp.zeros_like(q)
    qm = (jnp.where(even, q, zero), jnp.where(even, zero, q))
    nt = (((1,), (1,)), ((), ()))
    ref = lamv_ref[4:5, 0:1]

    l_scr[...] = jnp.zeros(l_scr.shape, F32)
    acc_scr[...] = jnp.zeros(acc_scr.shape, F32)

    def scores_softmax(kc, pb):
        rows = kc.shape[0]
        for mp in range(2):
            lsum = jnp.zeros((8, tq), F32)
            for r0 in range(0, rows, rb):
                s = lax.dot_general(kc[r0:r0 + rb, :], qm[mp], nt, preferred_element_type=F32)
                p = jnp.exp2(s - ref)
                lsum = lsum + jnp.sum(p.reshape(rb // 8, 8, tq), axis=0)
                p_scr[pb, mp, r0:r0 + rb, :] = p.astype(BF16)
            l_scr[mp] += lsum

    def values(vtc, pb):
        rows = vtc.shape[1]
        for mp in range(2):
            acc_scr[mp] += jnp.dot(vtc, p_scr[pb, mp, 0:rows, :], preferred_element_type=F32)

    units = tk // n_ctx

    def k_chunk(j):
        return k_ref[0, pl.ds(pl.multiple_of(n_ctx + (j - 1) * tk, n_ctx), tk), :]

    def vt_chunk(j):
        return jnp.concatenate([vt_ref[0, 0, 1 + (j - 1) * units + u] for u in range(units)], axis=1)

    scores_softmax(k_ref[0, 0:n_ctx, :], 0)
    if n_lat_chunks == 0:
        values(vt_ref[0, 0, 0], 0)
    else:
        assert n_lat_chunks % 2 == 0
        scores_softmax(k_chunk(1), 1)
        values(vt_ref[0, 0, 0], 0)

        def two_steps(t):
            scores_softmax(k_chunk(t), 0)
            values(vt_chunk(t - 1), 1)
            scores_softmax(k_chunk(t + 1), 1)
            values(vt_chunk(t), 0)

        n_pairs = (n_lat_chunks - 2) // 2
        n_trips = n_pairs // unroll

        def trip(i, carry):
            for u in range(unroll):
                two_steps(2 + 2 * (i * unroll + u))
            return carry

        if n_trips:
            lax.fori_loop(0, n_trips, trip, 0)
        for pr in range(n_trips * unroll, n_pairs):
            two_steps(2 + 2 * pr)
        scores_softmax(k_chunk(n_lat_chunks), 0)
        values(vt_chunk(n_lat_chunks - 1), 1)
        values(vt_chunk(n_lat_chunks), 0)

    lv = lamv_ref[...]
    lam = (jnp.exp(jnp.sum(lv[0:1] * lv[1:2], axis=1, keepdims=True))
           - jnp.exp(jnp.sum(lv[2:3] * lv[3:4], axis=1, keepdims=True)) + lam_init)
    l0 = jnp.sum(l_scr[0], axis=0, keepdims=True)
    l1 = jnp.sum(l_scr[1], axis=0, keepdims=True)
    ot = acc_scr[0] / l0 - lam * (acc_scr[1] / l1)
    o = ot.T
    o = o * lax.rsqrt(jnp.mean(o * o, axis=-1, keepdims=True) + EPS)
    o_ref[0] = (o * hg_ref[...] * (1.0 - lam_init)).astype(o_ref.dtype)


def _with_prev(kernel_fn, n_in):
    def wrapped(*refs, **kw):
        kernel_fn(*refs[:n_in], *refs[n_in + 1:], **kw)
    return wrapped


def _attention_calls(kernel_fn, lead, lead_specs, scratch_fn, q, k, vt, n_ctx, kw, name):
    b, n, dq = q.shape
    heads = dq // V_DIM
    tq = n_ctx
    s = n - n_ctx
    n_in = len(lead) + 3
    o = pl.pallas_call(
        functools.partial(kernel_fn, n_lat_chunks=s // kw["tk"], **kw),
        grid=(b, heads, s // tq),
        in_specs=lead_specs + [
            pl.BlockSpec((1, tq, V_DIM), lambda bi, h, i: (bi, i + 1, h)),
            pl.BlockSpec((1, n, V_DIM), lambda bi, h, i: (bi, 0, h)),
            pl.BlockSpec((1, 1, n // n_ctx, V_DIM, n_ctx), lambda bi, h, i: (bi, h, 0, 0, 0))],
        out_specs=pl.BlockSpec((1, tq, V_DIM), lambda bi, h, i: (bi, i + 1, h)),
        out_shape=jax.ShapeDtypeStruct((b, n, dq), BF16),
        scratch_shapes=scratch_fn(tq),
        compiler_params=_cparams(("parallel", "parallel", "arbitrary")),
        name=name,
    )(*lead, q, k, vt)
    return pl.pallas_call(
        functools.partial(_with_prev(kernel_fn, n_in), n_lat_chunks=0, **kw),
        grid=(b, heads, 1),
        in_specs=lead_specs + [
            pl.BlockSpec((1, tq, V_DIM), lambda bi, h, i: (bi, 0, h)),
            pl.BlockSpec((1, n_ctx, V_DIM), lambda bi, h, i: (bi, 0, h)),
            pl.BlockSpec((1, 1, 1, V_DIM, n_ctx), lambda bi, h, i: (bi, h, 0, 0, 0)),
            pl.BlockSpec(memory_space=pl.ANY)],
        out_specs=pl.BlockSpec((1, tq, V_DIM), lambda bi, h, i: (bi, 0, h)),
        out_shape=jax.ShapeDtypeStruct((b, n, dq), BF16),
        scratch_shapes=scratch_fn(tq),
        input_output_aliases={n_in: 0},
        compiler_params=_cparams(("parallel", "parallel", "arbitrary")),
        name=name + "_ctx",
    )(*lead, q, k, vt, o)


def _attention(lam_vecs, head_g, q, k, vt, n_ctx, lam_init, bound):
    b, n, dq = q.shape
    s = n - n_ctx
    tk = _pick_tile(s // 2, 1024, n_ctx)
    rb = _pick_tile(n_ctx, 128, 8)
    kw = dict(n_ctx=n_ctx, tk=tk, lam_init=lam_init, rb=rb)
    lamv = jnp.concatenate([lam_vecs, jnp.full((4, HEAD_DIM), bound, F32)], axis=0)
    spec2 = [pl.BlockSpec((8, HEAD_DIM), lambda bi, h, i: (0, 0)),
             pl.BlockSpec((1, V_DIM), lambda bi, h, i: (0, 0))]

    def robust_scratch(tq):
        return [pltpu.VMEM((2, 8, tq), F32), pltpu.VMEM((2, 8, tq), F32),
                pltpu.VMEM((2, V_DIM, tq), F32), pltpu.VMEM((2, 2, 8, tq), F32),
                pltpu.VMEM((2, 2, 8, tq), F32), pltpu.VMEM((2, 2, tk, tq), F32),
                pltpu.VMEM((2, 2, tk, tq), BF16)]

    def fast_scratch(tq):
        return [pltpu.VMEM((2, 8, tq), F32), pltpu.VMEM((2, V_DIM, tq), F32),
                pltpu.VMEM((2, 2, tk, tq), BF16)]

    def fast(args):
        return _attention_calls(_attn_fast_kernel, [lamv, head_g], spec2, fast_scratch,
                                *args, n_ctx, dict(kw, unroll=FAST_ATTN_UNROLL), "diff_attn")

    def robust(args):
        return _attention_calls(_attn_kernel, [lamv, head_g], spec2, robust_scratch,
                                *args, n_ctx, kw, "diff_attn_online")

    return lax.cond(bound <= SAFE_SCORE_BOUND, fast, robust, (q, k, vt))


def _dft_cos_sin(n, rows=None, cols=None, mod=None):
    mod = n if mod is None else mod
    r = jnp.arange(n if rows is None else rows, dtype=jnp.int32)[:, None]
    c = jnp.arange(n if cols is None else cols, dtype=jnp.int32)[None, :]
    ang = ((r * c) % mod).astype(F32) * (2.0 * math.pi / mod)
    return jnp.cos(ang), jnp.sin(ang)


def _channel_dft(x, cs_ref, groups):
    cg = x.shape[1] // groups
    wr, wi = [], []
    for g in range(groups):
        w = jnp.dot(x[:, g * cg:(g + 1) * cg], cs_ref[...], preferred_element_type=F32)
        wr.append(w[:, :cg])
        wi.append(w[:, cg:])
    return jnp.concatenate(wr, axis=1), jnp.concatenate(wi, axis=1)


def _fourier1_kernel(x_ref, cs_ref, kc_ref, ks_ref, tc_ref, ts_ref, zr_ref, zi_ref, *, groups):
    ls, tf, c = x_ref.shape[1:]
    x = x_ref[0].reshape(ls * tf, c)
    wr, wi = _channel_dft(x, cs_ref, groups)
    wr = wr.astype(BF16)
    wi = wi.astype(BF16)
    kc = kc_ref[...]
    ks = ks_ref[...]
    zr = jnp.dot(kc, wr, preferred_element_type=F32) + jnp.dot(ks, wi, preferred_element_type=F32)
    zi = jnp.dot(kc, wi, preferred_element_type=F32) - jnp.dot(ks, wr, preferred_element_type=F32)
    reps = c // V7X_LANES
    tc = jnp.concatenate([tc_ref[0]] * reps, axis=1)
    ts = jnp.concatenate([ts_ref[0]] * reps, axis=1)
    zr_ref[0] = (zr * tc + zi * ts).astype(BF16).reshape(ls, tf, c)
    zi_ref[0] = (zi * tc - zr * ts).astype(BF16).reshape(ls, tf, c)


def _fourier2_kernel(zr_ref, zi_ref, fc_ref, fs_ref, y_ref, *, scale):
    for u in range(zr_ref.shape[1]):
        y = (jnp.dot(fc_ref[...], zr_ref[0, u], preferred_element_type=F32)
             + jnp.dot(fs_ref[...], zi_ref[0, u], preferred_element_type=F32))
        y_ref[0, u] = (y * scale).astype(BF16)


def _fourier_ctx_kernel(x_ref, cs_ref, pc_ref, ps_ref, y_ref, *, groups, scale):
    wr, wi = _channel_dft(x_ref[0], cs_ref, groups)
    y = (jnp.dot(pc_ref[...], wr.astype(BF16), preferred_element_type=F32)
         + jnp.dot(ps_ref[...], wi.astype(BF16), preferred_element_type=F32))
    y_ref[0] = (y * scale).astype(BF16)


def _fourier_tables(s, c):
    cg = c // N_FOURIER_GROUPS
    cc, sc = _dft_cos_sin(cg)
    cs = jnp.concatenate([cc, -sc], axis=1).astype(BF16)
    return cg, cs


def _fourier_latent(fx, groups):
    b, s, c = fx.shape
    cg, cs = _fourier_tables(s, c)
    ls = FOURIER_SLOW
    lf = s // ls
    tf = V7X_BF16_SUBLANES
    assert ls * lf == s and lf % tf == 0
    rows = ls * tf
    pc, ps = _dft_cos_sin(ls)
    eye = jnp.eye(tf, dtype=F32)
    kc = jnp.kron(pc, eye).astype(BF16)
    ks = jnp.kron(ps, eye).astype(BF16)
    u_idx = jnp.repeat(jnp.arange(ls, dtype=jnp.int32), tf)[None, :]
    f_idx = (jnp.arange(lf // tf, dtype=jnp.int32)[:, None] * tf
             + jnp.tile(jnp.arange(tf, dtype=jnp.int32), ls)[None, :])
    ang = ((u_idx * f_idx) % s).astype(F32) * (2.0 * math.pi / s)
    tc = jnp.broadcast_to(jnp.cos(ang)[:, :, None], (lf // tf, rows, V7X_LANES))
    ts = jnp.broadcast_to(jnp.sin(ang)[:, :, None], (lf // tf, rows, V7X_LANES))

    x4 = fx.reshape(b, ls, lf, c)
    zshape = jax.ShapeDtypeStruct((b, ls, lf, c), BF16)
    blk = pl.BlockSpec((1, ls, tf, c), lambda bi, i: (bi, 0, i, 0))
    zr, zi = pl.pallas_call(
        functools.partial(_fourier1_kernel, groups=groups),
        grid=(b, lf // tf),
        in_specs=[blk,
                  pl.BlockSpec((cg, 2 * cg), lambda bi, i: (0, 0)),
                  pl.BlockSpec((rows, rows), lambda bi, i: (0, 0)),
                  pl.BlockSpec((rows, rows), lambda bi, i: (0, 0)),
                  pl.BlockSpec((1, rows, V7X_LANES), lambda bi, i: (i, 0, 0)),
                  pl.BlockSpec((1, rows, V7X_LANES), lambda bi, i: (i, 0, 0))],
        out_specs=[blk, blk],
        out_shape=[zshape, zshape],
        compiler_params=_cparams(("parallel", "parallel")),
        name="fourier_slow",
    )(x4, cs, kc, ks, tc, ts)

    fc, fs = _dft_cos_sin(lf)
    tu = _pick_tile(ls, 4, 1)
    blk2 = pl.BlockSpec((1, tu, lf, c), lambda bi, i: (bi, i, 0, 0))
    yt = pl.pallas_call(
        functools.partial(_fourier2_kernel, scale=1.0 / math.sqrt(s * cg)),
        grid=(b, ls // tu),
        in_specs=[blk2, blk2,
                  pl.BlockSpec((lf, lf), lambda bi, i: (0, 0)),
                  pl.BlockSpec((lf, lf), lambda bi, i: (0, 0))],
        out_specs=blk2,
        out_shape=zshape,
        compiler_params=_cparams(("parallel", "parallel")),
        name="fourier_fast",
    )(zr, zi, fc.astype(BF16), fs.astype(BF16))
    return jnp.swapaxes(yt, 1, 2).reshape(b, s, c)


def _fourier_ctx(fxc, groups):
    b, n_ctx, c = fxc.shape
    cg, cs = _fourier_tables(n_ctx, c)
    pc, ps = _dft_cos_sin(n_ctx)
    return pl.pallas_call(
        functools.partial(_fourier_ctx_kernel, groups=groups, scale=1.0 / math.sqrt(n_ctx * cg)),
        grid=(b,),
        in_specs=[pl.BlockSpec((1, n_ctx, c), lambda bi: (bi, 0, 0)),
                  pl.BlockSpec((cg, 2 * cg), lambda bi: (0, 0)),
                  pl.BlockSpec((n_ctx, n_ctx), lambda bi: (0, 0)),
                  pl.BlockSpec((n_ctx, n_ctx), lambda bi: (0, 0))],
        out_specs=pl.BlockSpec((1, n_ctx, c), lambda bi: (bi, 0, 0)),
        out_shape=jax.ShapeDtypeStruct((b, n_ctx, c), BF16),
        compiler_params=_cparams(("parallel",)),
        name="fourier_ctx",
    )(fxc, cs, pc.astype(BF16), ps.astype(BF16))


def _out_ab_kernel(x_ref, f_ref, o_ref_in, mod_ref, w_ref, out_ref, *, n_ctx, tm, c):
    (gate,) = _row_mods(mod_ref, pl.program_id(1) * tm, tm, n_ctx, (2,))
    y = (jnp.dot(f_ref[0], w_ref[0:c, :], preferred_element_type=F32)
         + jnp.dot(o_ref_in[0], w_ref[c:, :], preferred_element_type=F32))
    out_ref[0] = x_ref[0] + gate * y


def _out_ab(xc, f, o, mods, w_out, n_ctx):
    b, n, d = xc.shape
    c = f.shape[2]
    tm = _pick_tile(n, 640, 128)
    return pl.pallas_call(
        functools.partial(_out_ab_kernel, n_ctx=n_ctx, tm=tm, c=c),
        grid=(b, n // tm),
        in_specs=[pl.BlockSpec((1, tm, d), lambda bi, i: (bi, i, 0)),
                  pl.BlockSpec((1, tm, c), lambda bi, i: (bi, i, 0)),
                  pl.BlockSpec((1, tm, o.shape[2]), lambda bi, i: (bi, i, 0)),
                  pl.BlockSpec((1, 2, 6, d), lambda bi, i: (bi, 0, 0, 0)),
                  pl.BlockSpec(w_out.shape, lambda bi, i: (0, 0))],
        out_specs=pl.BlockSpec((1, tm, d), lambda bi, i: (bi, i, 0)),
        out_shape=jax.ShapeDtypeStruct((b, n, d), F32),
        compiler_params=_cparams(("parallel", "parallel")),
        name="out_proj_attn",
    )(xc, f, o, mods, w_out)


def _out_lru_kernel(x_ref, yf_ref, yr_ref, gate_ref, mod_ref, w_ref, out_ref, *, n_ctx, tm):
    (gate,) = _row_mods(mod_ref, pl.program_id(1) * tm, tm, n_ctx, (2,))
    z = (yf_ref[0].astype(F32) + yr_ref[0].astype(F32)) * _gelu_tanh(gate_ref[0].astype(F32))
    y = jnp.dot(z.astype(BF16), w_ref[...], preferred_element_type=F32)
    out_ref[0] = x_ref[0] + gate * y


def _out_lru(xc, yf, yr, u, mods, w_out, n_ctx):
    b, n, d = xc.shape
    tm = _pick_tile(n, 640, 64)
    row = pl.BlockSpec((1, tm, d), lambda bi, i: (bi, i, 0))
    return pl.pallas_call(
        functools.partial(_out_lru_kernel, n_ctx=n_ctx, tm=tm),
        grid=(b, n // tm),
        in_specs=[row, row, row, row,
                  pl.BlockSpec((1, 2, 6, d), lambda bi, i: (bi, 0, 0, 0)),
                  pl.BlockSpec(w_out.shape, lambda bi, i: (0, 0))],
        out_specs=row,
        out_shape=jax.ShapeDtypeStruct((b, n, d), F32),
        compiler_params=_cparams(("parallel", "parallel")),
        name="out_proj_lru",
    )(xc, yf, yr, u, mods, w_out)


def _conv_kernel(main_ref, prev_ref, next_ref, w_ref, b_ref, o_ref, ext_scr, *, n_blocks, halo):
    i = pl.program_id(1)
    tb = main_ref.shape[1]
    first = jnp.logical_or(i == 0, i == 1)
    last = jnp.logical_or(i == 0, i == n_blocks - 1)
    w = w_ref[...]
    bias = b_ref[...]
    zero = jnp.minimum(i, 0)
    for j in range(w.shape[1] // V7X_LANES):
        cols = slice(j * V7X_LANES, (j + 1) * V7X_LANES)
        ext_scr[j, 0:halo, :] = jnp.where(first, 0.0, prev_ref[0, :, cols].astype(F32))
        ext_scr[j, halo:halo + tb, :] = main_ref[0, :, cols].astype(F32)
        ext_scr[j, halo + tb:, :] = jnp.where(last, 0.0, next_ref[0, :, cols].astype(F32))
        rs = min(tb, 64)
        for r0 in range(0, tb, rs):
            acc = jnp.broadcast_to(bias[:, cols], (rs, V7X_LANES))
            for k in range(w.shape[0]):
                acc = acc + w[k:k + 1, cols] * ext_scr[j, pl.ds(halo + r0 + k - 2 + zero, rs), :]
            o_ref[0, r0:r0 + rs, cols] = acc


def _conv(u, conv_w, conv_b, n_ctx, col):
    b, n, d2 = u.shape
    d = d2 // 2
    tb = n_ctx
    halo = V7X_BF16_SUBLANES
    nb = n // tb
    hb = tb // halo
    return pl.pallas_call(
        functools.partial(_conv_kernel, n_blocks=nb, halo=halo),
        grid=(b, nb),
        in_specs=[pl.BlockSpec((1, tb, d), lambda bi, i: (bi, i, col)),
                  pl.BlockSpec((1, halo, d), lambda bi, i: (bi, jnp.maximum(i * hb - 1, 0), col)),
                  pl.BlockSpec((1, halo, d), lambda bi, i: (bi, jnp.minimum((i + 1) * hb, n // halo - 1), col)),
                  pl.BlockSpec(conv_w.shape, lambda bi, i: (0, 0)),
                  pl.BlockSpec((1, d), lambda bi, i: (0, 0))],
        out_specs=pl.BlockSpec((1, tb, d), lambda bi, i: (bi, i, 0)),
        out_shape=jax.ShapeDtypeStruct((b, n, d), F32),
        scratch_shapes=[pltpu.VMEM((d // V7X_LANES, tb + 2 * halo, V7X_LANES), F32)],
        compiler_params=_cparams(("parallel", "parallel")),
        name="lru_conv",
    )(u, u, u, conv_w, conv_b.reshape(1, d))


def _lru_scan_kernel(xf_ref, xr_ref, gaw_ref, gab_ref, gxw_ref, gxb_ref, lam_ref, yf_ref, yr_ref,
                     h_scr, a0_scr, a1_scr, b0_scr, b1_scr, y0_scr, y1_scr):
    t = pl.program_id(1)
    tb, d = xf_ref.shape[1:]
    nblk = gaw_ref.shape[1]
    bs = d // nblk
    nl = d // V7X_LANES
    pitch = SCAN_ROW_PITCH
    a_scr, b_scr, y_scr = (a0_scr, a1_scr), (b0_scr, b1_scr), (y0_scr, y1_scr)

    @pl.when(t == 0)
    def _():
        h_scr[...] = jnp.zeros(h_scr.shape, F32)

    def block_diag(xb, w_ref, dr):
        return jnp.concatenate(
            [jnp.dot(xb[:, n * bs:(n + 1) * bs], w_ref[dr, n], preferred_element_type=F32)
             for n in range(nblk)], axis=1)

    for dr, x_ref in enumerate((xf_ref, xr_ref)):
        xc = x_ref[0]
        xb = xc.astype(BF16)
        th_a = jnp.tanh(block_diag(xb, gaw_ref, dr) + gab_ref[dr:dr + 1, :])
        th_x = jnp.tanh(block_diag(xb, gxw_ref, dr) + gxb_ref[dr:dr + 1, :])
        z = -lam_ref[dr:dr + 1, :]
        softplus = jnp.maximum(z, 0.0) + jnp.log1p(jnp.exp(-jnp.abs(z)))
        c = (-0.5 * LRU_C) * softplus
        a = jnp.exp(c + c * th_a)
        w = 1.0 - a * a
        mult = jnp.where(w > 0.0, w * lax.rsqrt(w), 0.0)
        bb = mult * (0.5 * xc) * (1.0 + th_x)
        for j in range(nl):
            cols = slice(j * V7X_LANES, (j + 1) * V7X_LANES)
            a_scr[dr][pl.ds(j, tb, stride=pitch), :] = a[:, cols]
            b_scr[dr][pl.ds(j, tb, stride=pitch), :] = bb[:, cols]

    def step(i, carry):
        hf, hr = carry
        rf = i * pitch
        hf = a_scr[0][pl.ds(rf, nl), :] * hf + b_scr[0][pl.ds(rf, nl), :]
        y_scr[0][pl.ds(rf, nl), :] = hf
        rr = (tb - 1 - i) * pitch
        hr = a_scr[1][pl.ds(rr, nl), :] * hr + b_scr[1][pl.ds(rr, nl), :]
        y_scr[1][pl.ds(rr, nl), :] = hr
        return hf, hr

    hf, hr = lax.fori_loop(0, tb, step, (h_scr[0], h_scr[1]), unroll=8)
    h_scr[0] = hf
    h_scr[1] = hr
    for j in range(nl):
        cols = slice(j * V7X_LANES, (j + 1) * V7X_LANES)
        yf_ref[0, :, cols] = y_scr[0][pl.ds(j, tb, stride=pitch), :].astype(yf_ref.dtype)
        yr_ref[0, :, cols] = y_scr[1][pl.ds(j, tb, stride=pitch), :].astype(yr_ref.dtype)


def _lru_scan(xcv, ga_w, ga_b, gx_w, gx_b, lam, n_ctx):
    b, n, d = xcv.shape
    tb = n_ctx
    nt = n // tb
    fwd = pl.BlockSpec((1, tb, d), lambda bi, t: (bi, t, 0))
    rev = pl.BlockSpec((1, tb, d), lambda bi, t: (bi, jnp.where(t == 0, 0, nt - t), 0))
    wspec = pl.BlockSpec(ga_w.shape, lambda bi, t: (0, 0, 0, 0))
    vspec = pl.BlockSpec((2, d), lambda bi, t: (0, 0))
    out = jax.ShapeDtypeStruct((b, n, d), BF16)
    return pl.pallas_call(
        _lru_scan_kernel,
        grid=(b, nt),
        in_specs=[fwd, rev, wspec, vspec, wspec, vspec, vspec],
        out_specs=[fwd, rev],
        out_shape=[out, out],
        scratch_shapes=[pltpu.VMEM((2, d // V7X_LANES, V7X_LANES), F32)]
        + [pltpu.VMEM((tb * SCAN_ROW_PITCH, V7X_LANES), F32)] * 6,
        compiler_params=_cparams(("parallel", "arbitrary")),
        name="lru_scan",
    )(xcv, xcv, ga_w, ga_b, gx_w, gx_b, lam)


def _ffn_kernel(x_ref, mod_ref, g_ref, wg_ref, wu_ref, wd_ref, o_ref, h_scr, *, n_ctx, tm):
    k = pl.program_id(2)
    row0 = pl.program_id(1) * tm

    @pl.when(k == 0)
    def _():
        _norm_mod_into(h_scr, x_ref[0], g_ref[...], mod_ref, row0, n_ctx, 3, 4)
        o_ref[0] = jnp.zeros(o_ref.shape[1:], F32)

    h = h_scr[...]
    act = _silu(jnp.dot(h, wg_ref[...], preferred_element_type=F32)) * jnp.dot(
        h, wu_ref[...], preferred_element_type=F32)
    o_ref[0] += jnp.dot(act.astype(BF16), wd_ref[...], preferred_element_type=F32)

    @pl.when(k == pl.num_programs(2) - 1)
    def _():
        (gate,) = _row_mods(mod_ref, row0, tm, n_ctx, (5,))
        o_ref[0] = x_ref[0] + gate * o_ref[0]


def _ffn(xc, mods, g, wg, wu, wd, n_ctx):
    b, n, d = xc.shape
    dff = wg.shape[1]
    tm = _pick_tile(n, 832, 64)
    tf = _pick_tile(dff, 512, V7X_LANES)
    return pl.pallas_call(
        functools.partial(_ffn_kernel, n_ctx=n_ctx, tm=tm),
        grid=(b, n // tm, dff // tf),
        in_specs=[pl.BlockSpec((1, tm, d), lambda bi, i, k: (bi, i, 0)),
                  pl.BlockSpec((1, 2, 6, d), lambda bi, i, k: (bi, 0, 0, 0)),
                  pl.BlockSpec((1, d), lambda bi, i, k: (0, 0)),
                  pl.BlockSpec((d, tf), lambda bi, i, k: (0, k)),
                  pl.BlockSpec((d, tf), lambda bi, i, k: (0, k)),
                  pl.BlockSpec((tf, d), lambda bi, i, k: (k, 0))],
        out_specs=pl.BlockSpec((1, tm, d), lambda bi, i, k: (bi, i, 0)),
        out_shape=jax.ShapeDtypeStruct((b, n, d), F32),
        scratch_shapes=[pltpu.VMEM((tm, d), BF16)],
        compiler_params=_cparams(("parallel", "parallel", "arbitrary")),
        name="ffn",
    )(xc, mods, g.reshape(1, d), wg, wu, wd)


def _permute_qk_cols(w, heads):
    d = w.shape[0]
    w5 = w.reshape(d, heads, 2, HEAD_DIM // 2, 2)
    return jnp.transpose(w5, (0, 1, 4, 3, 2)).reshape(d, heads * V_DIM)


def _qk_gain_index():
    gain_idx = np.zeros(V_DIM, np.int32)
    for half in range(2):
        for j in range(HEAD_DIM // 2):
            for mp in range(2):
                gain_idx[half * HEAD_DIM + 2 * j + mp] = 2 * j + half
    return gain_idx


def _rope_tables(n_ctx, s):
    n_rows = s // GRID_W
    rows = jnp.repeat(jnp.arange(n_rows, dtype=F32), GRID_W, total_repeat_length=s)
    cols = jnp.tile(jnp.arange(GRID_W, dtype=F32), n_rows)
    n_axis = HEAD_DIM // 4
    inv = ROPE_THETA ** (-jnp.arange(n_axis, dtype=F32) / n_axis)
    ang = jnp.concatenate([rows[:, None] * inv, cols[:, None] * inv], axis=-1)
    cos = jnp.concatenate([jnp.ones((n_ctx, HEAD_DIM // 2), F32), jnp.cos(ang)], axis=0)
    sin = jnp.concatenate([jnp.zeros((n_ctx, HEAD_DIM // 2), F32), jnp.sin(ang)], axis=0)
    cos_h = jnp.repeat(cos, 2, axis=1)
    sin_h = jnp.repeat(sin, 2, axis=1)
    return jnp.concatenate([cos_h, cos_h], axis=1), jnp.concatenate([-sin_h, sin_h], axis=1)


def kernel(x, c, ctx, c_ctx, ada_w, ada_b, norm_mix_g, norm_ffn_g, ffn_w_gate, ffn_w_up, ffn_w_down, ab_w_in, ab_w_out, ab_q_norm_g, ab_k_norm_g, ab_lam_q1, ab_lam_k1, ab_lam_q2, ab_lam_k2, ab_head_norm_g, lru_w_in, lru_w_out, lru_conv_w, lru_conv_b, lru_gate_a_w, lru_gate_a_b, lru_gate_x_w, lru_gate_x_b, lru_lambda):
    b, s, d = x.shape
    n_ctx = ctx.shape[1]
    depth = ada_w.shape[0]
    c_f = d // 2
    heads = (d // 2) // V_DIM
    dq = heads * V_DIM

    cond8 = jnp.zeros((8, d), F32).at[:b].set(c).at[b].set(c_ctx)
    ada = _adaln(cond8, ada_w, ada_b).reshape(depth, 8, 6, d)
    mods_all = jnp.stack([jnp.broadcast_to(ada[:, b][:, None], (depth, b, 6, d)), ada[:, :b]], axis=2)

    gain_idx = _qk_gain_index()
    cos_t, sin_t = _rope_tables(n_ctx, s)

    xc = jnp.concatenate([ctx, x], axis=1)

    for l in range(depth):
        j = l // 2
        mods = mods_all[l]
        if l % 2 == 0:
            lam_init = 0.8 - 0.6 * math.exp(-0.3 * l)
            w_in = ab_w_in[j]
            w_in = jnp.concatenate([w_in[:, :c_f],
                                    _permute_qk_cols(w_in[:, c_f:c_f + dq], heads),
                                    _permute_qk_cols(w_in[:, c_f + dq:c_f + 2 * dq], heads),
                                    w_in[:, c_f + 2 * dq:]], axis=1).astype(BF16)
            u = _proj(xc, mods, norm_mix_g[l], w_in, n_ctx)
            gq = jnp.take(ab_q_norm_g[j], gain_idx).reshape(1, V_DIM)
            gk = jnp.take(ab_k_norm_g[j], gain_idx).reshape(1, V_DIM)
            q, k = _qk_prep(u, cos_t, sin_t, gq, gk, heads, c_f // dq, c_f // dq + 1)
            v = u[:, :, c_f + 2 * dq:]
            vt = jnp.transpose(v.reshape(b, (n_ctx + s) // n_ctx, n_ctx, heads, V_DIM), (0, 3, 1, 4, 2))
            lamv = jnp.stack([ab_lam_q1[j], ab_lam_k1[j], ab_lam_q2[j], ab_lam_k2[j]])
            bound = (HEAD_DIM * Q_SCALE * BOUND_MARGIN
                     * jnp.max(jnp.abs(ab_q_norm_g[j])) * jnp.max(jnp.abs(ab_k_norm_g[j])))
            o = _attention(lamv, ab_head_norm_g[j].reshape(1, V_DIM), q, k, vt, n_ctx, lam_init, bound)
            f_ctx = _fourier_ctx(u[:, :n_ctx, :c_f], N_FOURIER_GROUPS)
            f_lat = _fourier_latent(u[:, n_ctx:, :c_f], N_FOURIER_GROUPS)
            f = jnp.concatenate([f_ctx, f_lat], axis=1)
            xc = _out_ab(xc, f, o, mods, ab_w_out[j].astype(BF16), n_ctx)
        else:
            u = _proj(xc, mods, norm_mix_g[l], lru_w_in[j].astype(BF16), n_ctx)
            xcv = _conv(u, lru_conv_w[j], lru_conv_b[j], n_ctx, 1)
            yf, yr = _lru_scan(xcv, (0.5 * lru_gate_a_w[j]).astype(BF16), 0.5 * lru_gate_a_b[j],
                               (0.5 * lru_gate_x_w[j]).astype(BF16), 0.5 * lru_gate_x_b[j],
                               lru_lambda[j], n_ctx)
            xc = _out_lru(xc, yf, yr, u, mods, lru_w_out[j].astype(BF16), n_ctx)
        xc = _ffn(xc, mods, norm_ffn_g[l], ffn_w_gate[l].astype(BF16), ffn_w_up[l].astype(BF16),
                  ffn_w_down[l].astype(BF16), n_ctx)
    return xc[:, n_ctx:]
```

```python
import functools
import math

import jax
import jax.numpy as jnp
import numpy as np
from jax import lax
from jax.experimental import pallas as pl
from jax.experimental.pallas import tpu as pltpu

F32 = jnp.float32
BF16 = jnp.bfloat16

EPS = 1e-6
GRID_W = 64
ROPE_THETA = 10000.0
HEAD_DIM = 64
V_DIM = 2 * HEAD_DIM
N_FOURIER_GROUPS = 4
LRU_C = 8.0
LOG2E = 1.4426950408889634
Q_SCALE = (HEAD_DIM ** -0.5) * LOG2E

V7X_LANES = 128
V7X_BF16_SUBLANES = 16
V7X_VMEM_BYTES = 64 * 1024 * 1024
VMEM_LIMIT = V7X_VMEM_BYTES - 8 * 1024 * 1024

FOURIER_SLOW = 32
FAST_ATTN_UNROLL = 3
SCAN_ROW_PITCH = 20
NEG_BIG = -1e30
BOUND_MARGIN = 1.02
SAFE_SCORE_BOUND = 50.0


def _cparams(sem, flags=None):
    return pltpu.CompilerParams(dimension_semantics=sem, vmem_limit_bytes=VMEM_LIMIT, flags=flags)


def _pick_tile(n, target, mult):
    best = None
    for t in range(mult, min(n, target) + 1, mult):
        if n % t == 0:
            best = t
    assert best is not None, (n, target, mult)
    return best


def _row_mods(mod_ref, row0, rows, n_ctx, idx):
    r = row0 + lax.broadcasted_iota(jnp.int32, (rows, 1), 0)
    is_ctx = r < n_ctx
    return [jnp.where(is_ctx, mod_ref[0, 0, k:k + 1, :], mod_ref[0, 1, k:k + 1, :]) for k in idx]


def _norm_mod_into(h_scr, x, g, mod_ref, row0, n_ctx, i_shift, i_scale):
    y = x * lax.rsqrt(jnp.mean(x * x, axis=-1, keepdims=True) + EPS)

    @pl.when(row0 >= n_ctx)
    def _():
        a = g * (1.0 + mod_ref[0, 1, i_scale:i_scale + 1, :])
        h_scr[...] = (y * a + mod_ref[0, 1, i_shift:i_shift + 1, :]).astype(BF16)

    @pl.when(row0 < n_ctx)
    def _():
        shift, scale = _row_mods(mod_ref, row0, x.shape[0], n_ctx, (i_shift, i_scale))
        h_scr[...] = ((y * g) * (1.0 + scale) + shift).astype(BF16)


def _silu(x):
    return x * jax.nn.sigmoid(x)


def _gelu_tanh(x):
    return 0.5 * x * (1.0 + jnp.tanh(math.sqrt(2.0 / math.pi) * (x + 0.044715 * (x * x * x))))


def _adaln_kernel(cond_ref, w_ref, b_ref, o_ref):
    s = _silu(cond_ref[...])
    o_ref[0] = jnp.dot(s.astype(BF16), w_ref[0].astype(BF16),
                       preferred_element_type=F32) + b_ref[0]


def _adaln(cond8, ada_w, ada_b):
    depth, d, n6 = ada_w.shape
    tn = _pick_tile(n6, 1024, V7X_LANES)
    return pl.pallas_call(
        _adaln_kernel,
        grid=(depth, n6 // tn),
        in_specs=[pl.BlockSpec((8, d), lambda l, j: (0, 0)),
                  pl.BlockSpec((1, d, tn), lambda l, j: (l, 0, j)),
                  pl.BlockSpec((1, 1, tn), lambda l, j: (l, 0, j))],
        out_specs=pl.BlockSpec((1, 8, tn), lambda l, j: (l, 0, j)),
        out_shape=jax.ShapeDtypeStruct((depth, 8, n6), F32),
        compiler_params=_cparams(("parallel", "parallel")),
        name="adaln",
    )(cond8, ada_w, ada_b.reshape(depth, 1, n6))


def _proj_kernel(x_ref, mod_ref, g_ref, w_ref, o_ref, h_scr, *, n_ctx, tm):
    row0 = pl.program_id(1) * tm

    @pl.when(pl.program_id(2) == 0)
    def _():
        _norm_mod_into(h_scr, x_ref[0], g_ref[...], mod_ref, row0, n_ctx, 0, 1)

    o_ref[0] = jnp.dot(h_scr[...], w_ref[...], preferred_element_type=F32).astype(o_ref.dtype)


def _proj(xc, mods, g, w, n_ctx):
    b, n, d = xc.shape
    nout = w.shape[1]
    tm = _pick_tile(n, 1280, 128)
    tn = _pick_tile(nout, 1024, V7X_LANES)
    return pl.pallas_call(
        functools.partial(_proj_kernel, n_ctx=n_ctx, tm=tm),
        grid=(b, n // tm, nout // tn),
        in_specs=[pl.BlockSpec((1, tm, d), lambda bi, i, j: (bi, i, 0)),
                  pl.BlockSpec((1, 2, 6, d), lambda bi, i, j: (bi, 0, 0, 0)),
                  pl.BlockSpec((1, d), lambda bi, i, j: (0, 0)),
                  pl.BlockSpec((d, tn), lambda bi, i, j: (0, j))],
        out_specs=pl.BlockSpec((1, tm, tn), lambda bi, i, j: (bi, i, j)),
        out_shape=jax.ShapeDtypeStruct((b, n, nout), BF16),
        scratch_shapes=[pltpu.VMEM((tm, d), BF16)],
        compiler_params=_cparams(("parallel", "parallel", "arbitrary")),
        name="proj",
    )(xc, mods, g.reshape(1, d), w)


def _qk_prep_kernel(q_ref, k_ref, cos_ref, sin_ref, gq_ref, gk_ref, grp_ref, qo_ref, ko_ref, *, heads, q_scale):
    cos = cos_ref[...]
    sin = sin_ref[...]
    w2 = 2 * V_DIM
    for src, g_ref, dst, sc in ((q_ref, gq_ref, qo_ref, q_scale), (k_ref, gk_ref, ko_ref, 1.0)):
        g = jnp.concatenate([g_ref[...], g_ref[...]], axis=1)
        for hp in range(heads // 2):
            x = src[0, :, hp * w2:(hp + 1) * w2].astype(F32)
            t = x * x
            t_hi = t.astype(BF16)
            t_lo = (t - t_hi.astype(F32)).astype(BF16)
            ss = (jnp.dot(t_hi, grp_ref[...], preferred_element_type=F32)
                  + jnp.dot(t_lo, grp_ref[...], preferred_element_type=F32))
            y = x * lax.rsqrt(ss * (1.0 / HEAD_DIM) + EPS) * g
            for hh in range(2):
                yh = y[:, hh * V_DIM:(hh + 1) * V_DIM]
                yh = yh * cos + pltpu.roll(yh, 64, axis=1) * sin
                c0 = hp * w2 + hh * V_DIM
                dst[0, :, c0:c0 + V_DIM] = (yh * sc).astype(BF16)


def _qk_prep(u, cos_t, sin_t, gq, gk, heads, q_col, k_col):
    b, n, _ = u.shape
    dq = heads * V_DIM
    assert heads % 2 == 0
    tm = _pick_tile(n, 640, 128)
    out = jax.ShapeDtypeStruct((b, n, dq), BF16)
    lane = np.arange(2 * V_DIM)
    group = ((lane[:, None] // V_DIM == lane[None, :] // V_DIM)
             & (lane[:, None] % 2 == lane[None, :] % 2)).astype(np.float32)
    return pl.pallas_call(
        functools.partial(_qk_prep_kernel, heads=heads, q_scale=Q_SCALE),
        grid=(b, n // tm),
        in_specs=[pl.BlockSpec((1, tm, dq), lambda bi, i: (bi, i, q_col)),
                  pl.BlockSpec((1, tm, dq), lambda bi, i: (bi, i, k_col)),
                  pl.BlockSpec((tm, V_DIM), lambda bi, i: (i, 0)),
                  pl.BlockSpec((tm, V_DIM), lambda bi, i: (i, 0)),
                  pl.BlockSpec((1, V_DIM), lambda bi, i: (0, 0)),
                  pl.BlockSpec((1, V_DIM), lambda bi, i: (0, 0)),
                  pl.BlockSpec((2 * V_DIM, 2 * V_DIM), lambda bi, i: (0, 0))],
        out_specs=[pl.BlockSpec((1, tm, dq), lambda bi, i: (bi, i, 0)),
                   pl.BlockSpec((1, tm, dq), lambda bi, i: (bi, i, 0))],
        out_shape=[out, out],
        compiler_params=_cparams(("parallel", "parallel")),
        name="qk_prep",
    )(u, u, cos_t, sin_t, gq, gk, jnp.asarray(group, BF16))


def _attn_kernel(lamv_ref, hg_ref, q_ref, k_ref, vt_ref, o_ref, m_scr, l_scr, acc_scr, cm_scr, alpha_scr,
                 s_scr, p_scr,
                 *, n_ctx, tk, n_lat_chunks, lam_init, rb):
    q = q_ref[0]
    tq = q.shape[0]
    lane = lax.broadcasted_iota(jnp.int32, q.shape, 1)
    even = (lane % 2) == 0
    zero = jnp.zeros_like(q)
    qm = (jnp.where(even, q, zero), jnp.where(even, zero, q))

    m_scr[...] = jnp.full(m_scr.shape, NEG_BIG, F32)
    l_scr[...] = jnp.zeros(l_scr.shape, F32)
    acc_scr[...] = jnp.zeros(acc_scr.shape, F32)

    def scores(kc, sb):
        rows = kc.shape[0]
        for mp in range(2):
            s = lax.dot_general(kc, qm[mp], (((1,), (1,)), ((), ())), preferred_element_type=F32)
            s_scr[sb, mp, 0:rows, :] = s
            cm_scr[sb, mp] = jnp.broadcast_to(jnp.max(s, axis=0, keepdims=True), (8, tq))

    def softmax(sb, rows):
        for mp in range(2):
            m_old = m_scr[mp]
            m_new = jnp.maximum(m_old, cm_scr[sb, mp])
            m_scr[mp] = m_new
            alpha = jnp.exp2(m_old - m_new)
            alpha_scr[sb, mp] = alpha
            lsum = jnp.zeros((8, tq), F32)
            for r0 in range(0, rows, rb):
                p = jnp.exp2(s_scr[sb, mp, r0:r0 + rb, :] - m_new[0:1, :])
                lsum = lsum + jnp.sum(p.reshape(rb // 8, 8, tq), axis=0)
                p_scr[sb, mp, r0:r0 + rb, :] = p.astype(BF16)
            l_scr[mp] = alpha * l_scr[mp] + jnp.broadcast_to(jnp.sum(lsum, axis=0, keepdims=True), (8, tq))

    def values(vtc, sb):
        rows = vtc.shape[1]
        for mp in range(2):
            acc_scr[mp] = alpha_scr[sb, mp, 0:1, :] * acc_scr[mp] + jnp.dot(
                vtc, p_scr[sb, mp, 0:rows, :], preferred_element_type=F32)

    units = tk // n_ctx

    def k_chunk(j):
        return k_ref[0, pl.ds(pl.multiple_of(n_ctx + (j - 1) * tk, n_ctx), tk), :]

    def vt_chunk(j):
        return jnp.concatenate([vt_ref[0, 0, 1 + (j - 1) * units + u] for u in range(units)], axis=1)

    n_chunks = n_lat_chunks + 1
    scores(k_ref[0, 0:n_ctx, :], 0)
    if n_chunks == 1:
        softmax(0, n_ctx)
        values(vt_ref[0, 0, 0], 0)
    else:
        assert n_chunks % 2 == 1
        scores(k_chunk(1), 1)
        softmax(0, n_ctx)
        scores(k_chunk(2), 0)
        softmax(1, tk)
        values(vt_ref[0, 0, 0], 0)

        def two_steps(i, carry):
            t = 2 * i + 1
            softmax(0, tk)
            scores(k_chunk(t + 2), 1)
            values(vt_chunk(t), 1)
            softmax(1, tk)
            scores(k_chunk(t + 3), 0)
            values(vt_chunk(t + 1), 0)
            return carry

        lax.fori_loop(0, (n_chunks - 3) // 2, two_steps, 0)
        softmax(0, tk)
        values(vt_chunk(n_chunks - 2), 1)
        values(vt_chunk(n_chunks - 1), 0)

    lv = lamv_ref[...]
    lam = (jnp.exp(jnp.sum(lv[0:1] * lv[1:2], axis=1, keepdims=True))
           - jnp.exp(jnp.sum(lv[2:3] * lv[3:4], axis=1, keepdims=True)) + lam_init)
    ot = acc_scr[0] / l_scr[0, 0:1, :] - lam * (acc_scr[1] / l_scr[1, 0:1, :])
    o = ot.T
    o = o * lax.rsqrt(jnp.mean(o * o, axis=-1, keepdims=True) + EPS)
    o_ref[0] = (o * hg_ref[...] * (1.0 - lam_init)).astype(o_ref.dtype)


def _attn_fast_kernel(lamv_ref, hg_ref, q_ref, k_ref, vt_ref, o_ref, l_scr, acc_scr, p_scr,
                      *, n_ctx, tk, n_lat_chunks, lam_init, rb, unroll):
    q = q_ref[0]
    tq = q.shape[0]
    lane = lax.broadcasted_iota(jnp.int32, q.shape, 1)
    even = (lane % 2) == 0
    zero = jnp.zeros_like(q)
    qm = (jnp.where(even, q, zero), jnp.where(even, zero, q))
    nt = (((1,), (1,)), ((), ()))
    ref = lamv_ref[4:5, 0:1]

    l_scr[...] = jnp.zeros(l_scr.shape, F32)
    acc_scr[...] = jnp.zeros(acc_scr.shape, F32)

    def scores_softmax(kc, pb):
        rows = kc.shape[0]
        for mp in range(2):
            lsum = jnp.zeros((8, tq), F32)
            for r0 in range(0, rows, rb):
                s = lax.dot_general(kc[r0:r0 + rb, :], qm[mp], nt, preferred_element_type=F32)
                p = jnp.exp2(s - ref)
                lsum = lsum + jnp.sum(p.reshape(rb // 8, 8, tq), axis=0)
                p_scr[pb, mp, r0:r0 + rb, :] = p.astype(BF16)
            l_scr[mp] += lsum

    def values(vtc, pb):
        rows = vtc.shape[1]
        for mp in range(2):
            acc_scr[mp] += jnp.dot(vtc, p_scr[pb, mp, 0:rows, :], preferred_element_type=F32)

    units = tk // n_ctx

    def k_chunk(j):
        return k_ref[0, pl.ds(pl.multiple_of(n_ctx + (j - 1) * tk, n_ctx), tk), :]

    def vt_chunk(j):
        return jnp.concatenate([vt_ref[0, 0, 1 + (j - 1) * units + u] for u in range(units)], axis=1)

    scores_softmax(k_ref[0, 0:n_ctx, :], 0)
    if n_lat_chunks == 0:
        values(vt_ref[0, 0, 0], 0)
    else:
        assert n_lat_chunks % 2 == 0
        scores_softmax(k_chunk(1), 1)
        values(vt_ref[0, 0, 0], 0)

        def two_steps(t):
            scores_softmax(k_chunk(t), 0)
            values(vt_chunk(t - 1), 1)
            scores_softmax(k_chunk(t + 1), 1)
            values(vt_chunk(t), 0)

        n_pairs = (n_lat_chunks - 2) // 2
        n_trips = n_pairs // unroll

        def trip(i, carry):
            for u in range(unroll):
                two_steps(2 + 2 * (i * unroll + u))
            return carry

        if n_trips:
            lax.fori_loop(0, n_trips, trip, 0)
        for pr in range(n_trips * unroll, n_pairs):
            two_steps(2 + 2 * pr)
        scores_softmax(k_chunk(n_lat_chunks), 0)
        values(vt_chunk(n_lat_chunks - 1), 1)
        values(vt_chunk(n_lat_chunks), 0)

    lv = lamv_ref[...]
    lam = (jnp.exp(jnp.sum(lv[0:1] * lv[1:2], axis=1, keepdims=True))
           - jnp.exp(jnp.sum(lv[2:3] * lv[3:4], axis=1, keepdims=True)) + lam_init)
    l0 = jnp.sum(l_scr[0], axis=0, keepdims=True)
    l1 = jnp.sum(l_scr[1], axis=0, keepdims=True)
    ot = acc_scr[0] / l0 - lam * (acc_scr[1] / l1)
    o = ot.T
    o = o * lax.rsqrt(jnp.mean(o * o, axis=-1, keepdims=True) + EPS)
    o_ref[0] = (o * hg_ref[...] * (1.0 - lam_init)).astype(o_ref.dtype)


def _with_prev(kernel_fn, n_in):
    def wrapped(*refs, **kw):
        kernel_fn(*refs[:n_in], *refs[n_in + 1:], **kw)
    return wrapped


def _attention_calls(kernel_fn, lead, lead_specs, scratch_fn, q, k, vt, n_ctx, kw, name):
    b, n, dq = q.shape
    heads = dq // V_DIM
    tq = n_ctx
    s = n - n_ctx
    n_in = len(lead) + 3
    o = pl.pallas_call(
        functools.partial(kernel_fn, n_lat_chunks=s // kw["tk"], **kw),
        grid=(b, heads, s // tq),
        in_specs=lead_specs + [
            pl.BlockSpec((1, tq, V_DIM), lambda bi, h, i: (bi, i + 1, h)),
            pl.BlockSpec((1, n, V_DIM), lambda bi, h, i: (bi, 0, h)),
            pl.BlockSpec((1, 1, n // n_ctx, V_DIM, n_ctx), lambda bi, h, i: (bi, h, 0, 0, 0))],
        out_specs=pl.BlockSpec((1, tq, V_DIM), lambda bi, h, i: (bi, i + 1, h)),
        out_shape=jax.ShapeDtypeStruct((b, n, dq), BF16),
        scratch_shapes=scratch_fn(tq),
        compiler_params=_cparams(("parallel", "parallel", "arbitrary")),
        name=name,
    )(*lead, q, k, vt)
    return pl.pallas_call(
        functools.partial(_with_prev(kernel_fn, n_in), n_lat_chunks=0, **kw),
        grid=(b, heads, 1),
        in_specs=lead_specs + [
            pl.BlockSpec((1, tq, V_DIM), lambda bi, h, i: (bi, 0, h)),
            pl.BlockSpec((1, n_ctx, V_DIM), lambda bi, h, i: (bi, 0, h)),
            pl.BlockSpec((1, 1, 1, V_DIM, n_ctx), lambda bi, h, i: (bi, h, 0, 0, 0)),
            pl.BlockSpec(memory_space=pl.ANY)],
        out_specs=pl.BlockSpec((1, tq, V_DIM), lambda bi, h, i: (bi, 0, h)),
        out_shape=jax.ShapeDtypeStruct((b, n, dq), BF16),
        scratch_shapes=scratch_fn(tq),
        input_output_aliases={n_in: 0},
        compiler_params=_cparams(("parallel", "parallel", "arbitrary")),
        name=name + "_ctx",
    )(*lead, q, k, vt, o)


def _attention(lam_vecs, head_g, q, k, vt, n_ctx, lam_init, bound):
    b, n, dq = q.shape
    s = n - n_ctx
    tk = _pick_tile(s // 2, 1024, n_ctx)
    rb = _pick_tile(n_ctx, 128, 8)
    kw = dict(n_ctx=n_ctx, tk=tk, lam_init=lam_init, rb=rb)
    lamv = jnp.concatenate([lam_vecs, jnp.full((4, HEAD_DIM), bound, F32)], axis=0)
    spec2 = [pl.BlockSpec((8, HEAD_DIM), lambda bi, h, i: (0, 0)),
             pl.BlockSpec((1, V_DIM), lambda bi, h, i: (0, 0))]

    def robust_scratch(tq):
        return [pltpu.VMEM((2, 8, tq), F32), pltpu.VMEM((2, 8, tq), F32),
                pltpu.VMEM((2, V_DIM, tq), F32), pltpu.VMEM((2, 2, 8, tq), F32),
                pltpu.VMEM((2, 2, 8, tq), F32), pltpu.VMEM((2, 2, tk, tq), F32),
                pltpu.VMEM((2, 2, tk, tq), BF16)]

    def fast_scratch(tq):
        return [pltpu.VMEM((2, 8, tq), F32), pltpu.VMEM((2, V_DIM, tq), F32),
                pltpu.VMEM((2, 2, tk, tq), BF16)]

    def fast(args):
        return _attention_calls(_attn_fast_kernel, [lamv, head_g], spec2, fast_scratch,
                                *args, n_ctx, dict(kw, unroll=FAST_ATTN_UNROLL), "diff_attn")

    def robust(args):
        return _attention_calls(_attn_kernel, [lamv, head_g], spec2, robust_scratch,
                                *args, n_ctx, kw, "diff_attn_online")

    return lax.cond(bound <= SAFE_SCORE_BOUND, fast, robust, (q, k, vt))


def _dft_cos_sin(n, rows=None, cols=None, mod=None):
    mod = n if mod is None else mod
    r = jnp.arange(n if rows is None else rows, dtype=jnp.int32)[:, None]
    c = jnp.arange(n if cols is None else cols, dtype=jnp.int32)[None, :]
    ang = ((r * c) % mod).astype(F32) * (2.0 * math.pi / mod)
    return jnp.cos(ang), jnp.sin(ang)


def _channel_dft(x, cs_ref, groups):
    cg = x.shape[1] // groups
    wr, wi = [], []
    for g in range(groups):
        w = jnp.dot(x[:, g * cg:(g + 1) * cg], cs_ref[...], preferred_element_type=F32)
        wr.append(w[:, :cg])
        wi.append(w[:, cg:])
    return jnp.concatenate(wr, axis=1), jnp.concatenate(wi, axis=1)


def _fourier1_kernel(x_ref, cs_ref, kc_ref, ks_ref, tc_ref, ts_ref, zr_ref, zi_ref, *, groups):
    ls, tf, c = x_ref.shape[1:]
    x = x_ref[0].reshape(ls * tf, c)
    wr, wi = _channel_dft(x, cs_ref, groups)
    wr = wr.astype(BF16)
    wi = wi.astype(BF16)
    kc = kc_ref[...]
    ks = ks_ref[...]
    zr = jnp.dot(kc, wr, preferred_element_type=F32) + jnp.dot(ks, wi, preferred_element_type=F32)
    zi = jnp.dot(kc, wi, preferred_element_type=F32) - jnp.dot(ks, wr, preferred_element_type=F32)
    reps = c // V7X_LANES
    tc = jnp.concatenate([tc_ref[0]] * reps, axis=1)
    ts = jnp.concatenate([ts_ref[0]] * reps, axis=1)
    zr_ref[0] = (zr * tc + zi * ts).astype(BF16).reshape(ls, tf, c)
    zi_ref[0] = (zi * tc - zr * ts).astype(BF16).reshape(ls, tf, c)


def _fourier2_kernel(zr_ref, zi_ref, fc_ref, fs_ref, y_ref, *, scale):
    for u in range(zr_ref.shape[1]):
        y = (jnp.dot(fc_ref[...], zr_ref[0, u], preferred_element_type=F32)
             + jnp.dot(fs_ref[...], zi_ref[0, u], preferred_element_type=F32))
        y_ref[0, u] = (y * scale).astype(BF16)


def _fourier_ctx_kernel(x_ref, cs_ref, pc_ref, ps_ref, y_ref, *, groups, scale):
    wr, wi = _channel_dft(x_ref[0], cs_ref, groups)
    y = (jnp.dot(pc_ref[...], wr.astype(BF16), preferred_element_type=F32)
         + jnp.dot(ps_ref[...], wi.astype(BF16), preferred_element_type=F32))
    y_ref[0] = (y * scale).astype(BF16)


def _fourier_tables(s, c):
    cg = c // N_FOURIER_GROUPS
    cc, sc = _dft_cos_sin(cg)
    cs = jnp.concatenate([cc, -sc], axis=1).astype(BF16)
    return cg, cs


def _fourier_latent(fx, groups):
    b, s, c = fx.shape
    cg, cs = _fourier_tables(s, c)
    ls = FOURIER_SLOW
    lf = s // ls
    tf = V7X_BF16_SUBLANES
    assert ls * lf == s and lf % tf == 0
    rows = ls * tf
    pc, ps = _dft_cos_sin(ls)
    eye = jnp.eye(tf, dtype=F32)
    kc = jnp.kron(pc, eye).astype(BF16)
    ks = jnp.kron(ps, eye).astype(BF16)
    u_idx = jnp.repeat(jnp.arange(ls, dtype=jnp.int32), tf)[None, :]
    f_idx = (jnp.arange(lf // tf, dtype=jnp.int32)[:, None] * tf
             + jnp.tile(jnp.arange(tf, dtype=jnp.int32), ls)[None, :])
    ang = ((u_idx * f_idx) % s).astype(F32) * (2.0 * math.pi / s)
    tc = jnp.broadcast_to(jnp.cos(ang)[:, :, None], (lf // tf, rows, V7X_LANES))
    ts = jnp.broadcast_to(jnp.sin(ang)[:, :, None], (lf // tf, rows, V7X_LANES))

    x4 = fx.reshape(b, ls, lf, c)
    zshape = jax.ShapeDtypeStruct((b, ls, lf, c), BF16)
    blk = pl.BlockSpec((1, ls, tf, c), lambda bi, i: (bi, 0, i, 0))
    zr, zi = pl.pallas_call(
        functools.partial(_fourier1_kernel, groups=groups),
        grid=(b, lf // tf),
        in_specs=[blk,
                  pl.BlockSpec((cg, 2 * cg), lambda bi, i: (0, 0)),
                  pl.BlockSpec((rows, rows), lambda bi, i: (0, 0)),
                  pl.BlockSpec((rows, rows), lambda bi, i: (0, 0)),
                  pl.BlockSpec((1, rows, V7X_LANES), lambda bi, i: (i, 0, 0)),
                  pl.BlockSpec((1, rows, V7X_LANES), lambda bi, i: (i, 0, 0))],
        out_specs=[blk, blk],
        out_shape=[zshape, zshape],
        compiler_params=_cparams(("parallel", "parallel")),
        name="fourier_slow",
    )(x4, cs, kc, ks, tc, ts)

    fc, fs = _dft_cos_sin(lf)
    tu = _pick_tile(ls, 4, 1)
    blk2 = pl.BlockSpec((1, tu, lf, c), lambda bi, i: (bi, i, 0, 0))
    yt = pl.pallas_call(
        functools.partial(_fourier2_kernel, scale=1.0 / math.sqrt(s * cg)),
        grid=(b, ls // tu),
        in_specs=[blk2, blk2,
                  pl.BlockSpec((lf, lf), lambda bi, i: (0, 0)),
                  pl.BlockSpec((lf, lf), lambda bi, i: (0, 0))],
        out_specs=blk2,
        out_shape=zshape,
        compiler_params=_cparams(("parallel", "parallel")),
        name="fourier_fast",
    )(zr, zi, fc.astype(BF16), fs.astype(BF16))
    return jnp.swapaxes(yt, 1, 2).reshape(b, s, c)


def _fourier_ctx(fxc, groups):
    b, n_ctx, c = fxc.shape
    cg, cs = _fourier_tables(n_ctx, c)
    pc, ps = _dft_cos_sin(n_ctx)
    return pl.pallas_call(
        functools.partial(_fourier_ctx_kernel, groups=groups, scale=1.0 / math.sqrt(n_ctx * cg)),
        grid=(b,),
        in_specs=[pl.BlockSpec((1, n_ctx, c), lambda bi: (bi, 0, 0)),
                  pl.BlockSpec((cg, 2 * cg), lambda bi: (0, 0)),
                  pl.BlockSpec((n_ctx, n_ctx), lambda bi: (0, 0)),
                  pl.BlockSpec((n_ctx, n_ctx), lambda bi: (0, 0))],
        out_specs=pl.BlockSpec((1, n_ctx, c), lambda bi: (bi, 0, 0)),
        out_shape=jax.ShapeDtypeStruct((b, n_ctx, c), BF16),
        compiler_params=_cparams(("parallel",)),
        name="fourier_ctx",
    )(fxc, cs, pc.astype(BF16), ps.astype(BF16))


def _out_ab_kernel(x_ref, f_ref, o_ref_in, mod_ref, w_ref, out_ref, *, n_ctx, tm, c):
    (gate,) = _row_mods(mod_ref, pl.program_id(1) * tm, tm, n_ctx, (2,))
    y = (jnp.dot(f_ref[0], w_ref[0:c, :], preferred_element_type=F32)
         + jnp.dot(o_ref_in[0], w_ref[c:, :], preferred_element_type=F32))
    out_ref[0] = x_ref[0] + gate * y


def _out_ab(xc, f, o, mods, w_out, n_ctx):
    b, n, d = xc.shape
    c = f.shape[2]
    tm = _pick_tile(n, 640, 128)
    return pl.pallas_call(
        functools.partial(_out_ab_kernel, n_ctx=n_ctx, tm=tm, c=c),
        grid=(b, n // tm),
        in_specs=[pl.BlockSpec((1, tm, d), lambda bi, i: (bi, i, 0)),
                  pl.BlockSpec((1, tm, c), lambda bi, i: (bi, i, 0)),
                  pl.BlockSpec((1, tm, o.shape[2]), lambda bi, i: (bi, i, 0)),
                  pl.BlockSpec((1, 2, 6, d), lambda bi, i: (bi, 0, 0, 0)),
                  pl.BlockSpec(w_out.shape, lambda bi, i: (0, 0))],
        out_specs=pl.BlockSpec((1, tm, d), lambda bi, i: (bi, i, 0)),
        out_shape=jax.ShapeDtypeStruct((b, n, d), F32),
        compiler_params=_cparams(("parallel", "parallel")),
        name="out_proj_attn",
    )(xc, f, o, mods, w_out)


def _out_lru_kernel(x_ref, yf_ref, yr_ref, gate_ref, mod_ref, w_ref, out_ref, *, n_ctx, tm):
    (gate,) = _row_mods(mod_ref, pl.program_id(1) * tm, tm, n_ctx, (2,))
    z = (yf_ref[0].astype(F32) + yr_ref[0].astype(F32)) * _gelu_tanh(gate_ref[0].astype(F32))
    y = jnp.dot(z.astype(BF16), w_ref[...], preferred_element_type=F32)
    out_ref[0] = x_ref[0] + gate * y


def _out_lru(xc, yf, yr, u, mods, w_out, n_ctx):
    b, n, d = xc.shape
    tm = _pick_tile(n, 640, 64)
    row = pl.BlockSpec((1, tm, d), lambda bi, i: (bi, i, 0))
    return pl.pallas_call(
        functools.partial(_out_lru_kernel, n_ctx=n_ctx, tm=tm),
        grid=(b, n // tm),
        in_specs=[row, row, row, row,
                  pl.BlockSpec((1, 2, 6, d), lambda bi, i: (bi, 0, 0, 0)),
                  pl.BlockSpec(w_out.shape, lambda bi, i: (0, 0))],
        out_specs=row,
        out_shape=jax.ShapeDtypeStruct((b, n, d), F32),
        compiler_params=_cparams(("parallel", "parallel")),
        name="out_proj_lru",
    )(xc, yf, yr, u, mods, w_out)


def _conv_kernel(main_ref, prev_ref, next_ref, w_ref, b_ref, o_ref, ext_scr, *, n_blocks, halo):
    i = pl.program_id(1)
    tb = main_ref.shape[1]
    first = jnp.logical_or(i == 0, i == 1)
    last = jnp.logical_or(i == 0, i == n_blocks - 1)
    w = w_ref[...]
    bias = b_ref[...]
    zero = jnp.minimum(i, 0)
    for j in range(w.shape[1] // V7X_LANES):
        cols = slice(j * V7X_LANES, (j + 1) * V7X_LANES)
        ext_scr[j, 0:halo, :] = jnp.where(first, 0.0, prev_ref[0, :, cols].astype(F32))
        ext_scr[j, halo:halo + tb, :] = main_ref[0, :, cols].astype(F32)
        ext_scr[j, halo + tb:, :] = jnp.where(last, 0.0, next_ref[0, :, cols].astype(F32))
        rs = min(tb, 64)
        for r0 in range(0, tb, rs):
            acc = jnp.broadcast_to(bias[:, cols], (rs, V7X_LANES))
            for k in range(w.shape[0]):
                acc = acc + w[k:k + 1, cols] * ext_scr[j, pl.ds(halo + r0 + k - 2 + zero, rs), :]
            o_ref[0, r0:r0 + rs, cols] = acc


def _conv(u, conv_w, conv_b, n_ctx, col):
    b, n, d2 = u.shape
    d = d2 // 2
    tb = n_ctx
    halo = V7X_BF16_SUBLANES
    nb = n // tb
    hb = tb // halo
    return pl.pallas_call(
        functools.partial(_conv_kernel, n_blocks=nb, halo=halo),
        grid=(b, nb),
        in_specs=[pl.BlockSpec((1, tb, d), lambda bi, i: (bi, i, col)),
                  pl.BlockSpec((1, halo, d), lambda bi, i: (bi, jnp.maximum(i * hb - 1, 0), col)),
                  pl.BlockSpec((1, halo, d), lambda bi, i: (bi, jnp.minimum((i + 1) * hb, n // halo - 1), col)),
                  pl.BlockSpec(conv_w.shape, lambda bi, i: (0, 0)),
                  pl.BlockSpec((1, d), lambda bi, i: (0, 0))],
        out_specs=pl.BlockSpec((1, tb, d), lambda bi, i: (bi, i, 0)),
        out_shape=jax.ShapeDtypeStruct((b, n, d), F32),
        scratch_shapes=[pltpu.VMEM((d // V7X_LANES, tb + 2 * halo, V7X_LANES), F32)],
        compiler_params=_cparams(("parallel", "parallel")),
        name="lru_conv",
    )(u, u, u, conv_w, conv_b.reshape(1, d))


def _lru_scan_kernel(xf_ref, xr_ref, gaw_ref, gab_ref, gxw_ref, gxb_ref, lam_ref, yf_ref, yr_ref,
                     h_scr, a0_scr, a1_scr, b0_scr, b1_scr, y0_scr, y1_scr):
    t = pl.program_id(1)
    tb, d = xf_ref.shape[1:]
    nblk = gaw_ref.shape[1]
    bs = d // nblk
    nl = d // V7X_LANES
    pitch = SCAN_ROW_PITCH
    a_scr, b_scr, y_scr = (a0_scr, a1_scr), (b0_scr, b1_scr), (y0_scr, y1_scr)

    @pl.when(t == 0)
    def _():
        h_scr[...] = jnp.zeros(h_scr.shape, F32)

    assert bs % V7X_LANES == 0
    for dr, x_ref in enumerate((xf_ref, xr_ref)):
        for n in range(nblk):
            blk = slice(n * bs, (n + 1) * bs)
            xc = x_ref[0, :, blk]
            xb = xc.astype(BF16)
            th_a = jnp.tanh(jnp.dot(xb, gaw_ref[dr, n], preferred_element_type=F32) + gab_ref[dr:dr + 1, blk])
            th_x = jnp.tanh(jnp.dot(xb, gxw_ref[dr, n], preferred_element_type=F32) + gxb_ref[dr:dr + 1, blk])
            z = -lam_ref[dr:dr + 1, blk]
            softplus = jnp.maximum(z, 0.0) + jnp.log1p(jnp.exp(-jnp.abs(z)))
            c = (-0.5 * LRU_C) * softplus
            a = jnp.exp(c + c * th_a)
            w = 1.0 - a * a
            mult = jnp.where(w > 0.0, w * lax.rsqrt(w), 0.0)
            bb = mult * (0.5 * xc) * (1.0 + th_x)
            for jj in range(bs // V7X_LANES):
                j = n * (bs // V7X_LANES) + jj
                cols = slice(jj * V7X_LANES, (jj + 1) * V7X_LANES)
                a_scr[dr][pl.ds(j, tb, stride=pitch), :] = a[:, cols]
                b_scr[dr][pl.ds(j, tb, stride=pitch), :] = bb[:, cols]

    def step(i, carry):
        hf, hr = carry
        rf = i * pitch
        hf = a_scr[0][pl.ds(rf, nl), :] * hf + b_scr[0][pl.ds(rf, nl), :]
        y_scr[0][pl.ds(rf, nl), :] = hf
        rr = (tb - 1 - i) * pitch
        hr = a_scr[1][pl.ds(rr, nl), :] * hr + b_scr[1][pl.ds(rr, nl), :]
        y_scr[1][pl.ds(rr, nl), :] = hr
        return hf, hr

    hf, hr = lax.fori_loop(0, tb, step, (h_scr[0], h_scr[1]), unroll=8)
    h_scr[0] = hf
    h_scr[1] = hr
    for j in range(nl):
        cols = slice(j * V7X_LANES, (j + 1) * V7X_LANES)
        yf_ref[0, :, cols] = y_scr[0][pl.ds(j, tb, stride=pitch), :].astype(yf_ref.dtype)
        yr_ref[0, :, cols] = y_scr[1][pl.ds(j, tb, stride=pitch), :].astype(yr_ref.dtype)


def _lru_scan(xcv, ga_w, ga_b, gx_w, gx_b, lam, n_ctx):
    b, n, d = xcv.shape
    tb = n_ctx
    nt = n // tb
    fwd = pl.BlockSpec((1, tb, d), lambda bi, t: (bi, t, 0))
    rev = pl.BlockSpec((1, tb, d), lambda bi, t: (bi, jnp.where(t == 0, 0, nt - t), 0))
    wspec = pl.BlockSpec(ga_w.shape, lambda bi, t: (0, 0, 0, 0))
    vspec = pl.BlockSpec((2, d), lambda bi, t: (0, 0))
    out = jax.ShapeDtypeStruct((b, n, d), BF16)
    return pl.pallas_call(
        _lru_scan_kernel,
        grid=(b, nt),
        in_specs=[fwd, rev, wspec, vspec, wspec, vspec, vspec],
        out_specs=[fwd, rev],
        out_shape=[out, out],
        scratch_shapes=[pltpu.VMEM((2, d // V7X_LANES, V7X_LANES), F32)]
        + [pltpu.VMEM((tb * SCAN_ROW_PITCH, V7X_LANES), F32)] * 6,
        compiler_params=_cparams(("parallel", "arbitrary")),
        name="lru_scan",
    )(xcv, xcv, ga_w, ga_b, gx_w, gx_b, lam)


def _ffn_kernel(x_ref, mod_ref, g_ref, wg_ref, wu_ref, wd_ref, o_ref, h_scr, *, n_ctx, tm):
    k = pl.program_id(2)
    row0 = pl.program_id(1) * tm

    @pl.when(k == 0)
    def _():
        _norm_mod_into(h_scr, x_ref[0], g_ref[...], mod_ref, row0, n_ctx, 3, 4)
        o_ref[0] = jnp.zeros(o_ref.shape[1:], F32)

    h = h_scr[...]
    act = _silu(jnp.dot(h, wg_ref[...], preferred_element_type=F32)) * jnp.dot(
        h, wu_ref[...], preferred_element_type=F32)
    o_ref[0] += jnp.dot(act.astype(BF16), wd_ref[...], preferred_element_type=F32)

    @pl.when(k == pl.num_programs(2) - 1)
    def _():
        (gate,) = _row_mods(mod_ref, row0, tm, n_ctx, (5,))
        o_ref[0] = x_ref[0] + gate * o_ref[0]


def _ffn(xc, mods, g, wg, wu, wd, n_ctx):
    b, n, d = xc.shape
    dff = wg.shape[1]
    tm = _pick_tile(n, 832, 64)
    tf = _pick_tile(dff, 512, V7X_LANES)
    return pl.pallas_call(
        functools.partial(_ffn_kernel, n_ctx=n_ctx, tm=tm),
        grid=(b, n // tm, dff // tf),
        in_specs=[pl.BlockSpec((1, tm, d), lambda bi, i, k: (bi, i, 0)),
                  pl.BlockSpec((1, 2, 6, d), lambda bi, i, k: (bi, 0, 0, 0)),
                  pl.BlockSpec((1, d), lambda bi, i, k: (0, 0)),
                  pl.BlockSpec((d, tf), lambda bi, i, k: (0, k)),
                  pl.BlockSpec((d, tf), lambda bi, i, k: (0, k)),
                  pl.BlockSpec((tf, d), lambda bi, i, k: (k, 0))],
        out_specs=pl.BlockSpec((1, tm, d), lambda bi, i, k: (bi, i, 0)),
        out_shape=jax.ShapeDtypeStruct((b, n, d), F32),
        scratch_shapes=[pltpu.VMEM((tm, d), BF16)],
        compiler_params=_cparams(("parallel", "parallel", "arbitrary")),
        name="ffn",
    )(xc, mods, g.reshape(1, d), wg, wu, wd)


def _permute_qk_cols(w, heads):
    d = w.shape[0]
    w5 = w.reshape(d, heads, 2, HEAD_DIM // 2, 2)
    return jnp.transpose(w5, (0, 1, 4, 3, 2)).reshape(d, heads * V_DIM)


def _qk_gain_index():
    gain_idx = np.zeros(V_DIM, np.int32)
    for half in range(2):
        for j in range(HEAD_DIM // 2):
            for mp in range(2):
                gain_idx[half * HEAD_DIM + 2 * j + mp] = 2 * j + half
    return gain_idx


def _rope_tables(n_ctx, s):
    n_rows = s // GRID_W
    rows = jnp.repeat(jnp.arange(n_rows, dtype=F32), GRID_W, total_repeat_length=s)
    cols = jnp.tile(jnp.arange(GRID_W, dtype=F32), n_rows)
    n_axis = HEAD_DIM // 4
    inv = ROPE_THETA ** (-jnp.arange(n_axis, dtype=F32) / n_axis)
    ang = jnp.concatenate([rows[:, None] * inv, cols[:, None] * inv], axis=-1)
    cos = jnp.concatenate([jnp.ones((n_ctx, HEAD_DIM // 2), F32), jnp.cos(ang)], axis=0)
    sin = jnp.concatenate([jnp.zeros((n_ctx, HEAD_DIM // 2), F32), jnp.sin(ang)], axis=0)
    cos_h = jnp.repeat(cos, 2, axis=1)
    sin_h = jnp.repeat(sin, 2, axis=1)
    return jnp.concatenate([cos_h, cos_h], axis=1), jnp.concatenate([-sin_h, sin_h], axis=1)


def kernel(x, c, ctx, c_ctx, ada_w, ada_b, norm_mix_g, norm_ffn_g, ffn_w_gate, ffn_w_up, ffn_w_down, ab_w_in, ab_w_out, ab_q_norm_g, ab_k_norm_g, ab_lam_q1, ab_lam_k1, ab_lam_q2, ab_lam_k2, ab_head_norm_g, lru_w_in, lru_w_out, lru_conv_w, lru_conv_b, lru_gate_a_w, lru_gate_a_b, lru_gate_x_w, lru_gate_x_b, lru_lambda):
    b, s, d = x.shape
    n_ctx = ctx.shape[1]
    depth = ada_w.shape[0]
    c_f = d // 2
    heads = (d // 2) // V_DIM
    dq = heads * V_DIM

    cond8 = jnp.zeros((8, d), F32).at[:b].set(c).at[b].set(c_ctx)
    ada = _adaln(cond8, ada_w, ada_b).reshape(depth, 8, 6, d)
    mods_all = jnp.stack([jnp.broadcast_to(ada[:, b][:, None], (depth, b, 6, d)), ada[:, :b]], axis=2)

    gain_idx = _qk_gain_index()
    cos_t, sin_t = _rope_tables(n_ctx, s)

    xc = jnp.concatenate([ctx, x], axis=1)

    for l in range(depth):
        j = l // 2
        mods = mods_all[l]
        if l % 2 == 0:
            lam_init = 0.8 - 0.6 * math.exp(-0.3 * l)
            w_in = ab_w_in[j]
            w_in = jnp.concatenate([w_in[:, :c_f],
                                    _permute_qk_cols(w_in[:, c_f:c_f + dq], heads),
                                    _permute_qk_cols(w_in[:, c_f + dq:c_f + 2 * dq], heads),
                                    w_in[:, c_f + 2 * dq:]], axis=1).astype(BF16)
            u = _proj(xc, mods, norm_mix_g[l], w_in, n_ctx)
            gq = jnp.take(ab_q_norm_g[j], gain_idx).reshape(1, V_DIM)
            gk = jnp.take(ab_k_norm_g[j], gain_idx).reshape(1, V_DIM)
            q, k = _qk_prep(u, cos_t, sin_t, gq, gk, heads, c_f // dq, c_f // dq + 1)
            v = u[:, :, c_f + 2 * dq:]
            vt = jnp.transpose(v.reshape(b, (n_ctx + s) // n_ctx, n_ctx, heads, V_DIM), (0, 3, 1, 4, 2))
            lamv = jnp.stack([ab_lam_q1[j], ab_lam_k1[j], ab_lam_q2[j], ab_lam_k2[j]])
            bound = (HEAD_DIM * Q_SCALE * BOUND_MARGIN
                     * jnp.max(jnp.abs(ab_q_norm_g[j])) * jnp.max(jnp.abs(ab_k_norm_g[j])))
            o = _attention(lamv, ab_head_norm_g[j].reshape(1, V_DIM), q, k, vt, n_ctx, lam_init, bound)
            f_ctx = _fourier_ctx(u[:, :n_ctx, :c_f], N_FOURIER_GROUPS)
            f_lat = _fourier_latent(u[:, n_ctx:, :c_f], N_FOURIER_GROUPS)
            f = jnp.concatenate([f_ctx, f_lat], axis=1)
            xc = _out_ab(xc, f, o, mods, ab_w_out[j].astype(BF16), n_ctx)
        else:
            u = _proj(xc, mods, norm_mix_g[l], lru_w_in[j].astype(BF16), n_ctx)
            xcv = _conv(u, lru_conv_w[j], lru_conv_b[j], n_ctx, 1)
            yf, yr = _lru_scan(xcv, (0.5 * lru_gate_a_w[j]).astype(BF16), 0.5 * lru_gate_a_b[j],
                               (0.5 * lru_gate_x_w[j]).astype(BF16), 0.5 * lru_gate_x_b[j],
                               lru_lambda[j], n_ctx)
            xc = _out_lru(xc, yf, yr, u, mods, lru_w_out[j].astype(BF16), n_ctx)
        xc = _ffn(xc, mods, norm_ffn_g[l], ffn_w_gate[l].astype(BF16), ffn_w_up[l].astype(BF16),
                  ffn_w_down[l].astype(BF16), n_ctx)
    return xc[:, n_ctx:]
```

```python
import functools
import math

import jax
import jax.numpy as jnp
import numpy as np
from jax import lax
from jax.experimental import pallas as pl
from jax.experimental.pallas import tpu as pltpu

F32 = jnp.float32
BF16 = jnp.bfloat16

EPS = 1e-6
GRID_W = 64
ROPE_THETA = 10000.0
HEAD_DIM = 64
V_DIM = 2 * HEAD_DIM
N_FOURIER_GROUPS = 4
LRU_C = 8.0
LOG2E = 1.4426950408889634
Q_SCALE = (HEAD_DIM ** -0.5) * LOG2E

V7X_LANES = 128
V7X_BF16_SUBLANES = 16
V7X_VMEM_BYTES = 64 * 1024 * 1024
VMEM_LIMIT = V7X_VMEM_BYTES - 8 * 1024 * 1024

FOURIER_SLOW = 32
FAST_ATTN_UNROLL = 3
SCAN_ROW_PITCH = 20
NEG_BIG = -1e30
BOUND_MARGIN = 1.02
SAFE_SCORE_BOUND = 50.0


def _cparams(sem, flags=None):
    return pltpu.CompilerParams(dimension_semantics=sem, vmem_limit_bytes=VMEM_LIMIT, flags=flags)


def _pick_tile(n, target, mult):
    best = None
    for t in range(mult, min(n, target) + 1, mult):
        if n % t == 0:
            best = t
    assert best is not None, (n, target, mult)
    return best


def _row_mods(mod_ref, row0, rows, n_ctx, idx):
    r = row0 + lax.broadcasted_iota(jnp.int32, (rows, 1), 0)
    is_ctx = r < n_ctx
    return [jnp.where(is_ctx, mod_ref[0, 0, k:k + 1, :], mod_ref[0, 1, k:k + 1, :]) for k in idx]


def _norm_mod_into(h_scr, x, g, mod_ref, row0, n_ctx, i_shift, i_scale):
    y = x * lax.rsqrt(jnp.mean(x * x, axis=-1, keepdims=True) + EPS)

    @pl.when(row0 >= n_ctx)
    def _():
        a = g * (1.0 + mod_ref[0, 1, i_scale:i_scale + 1, :])
        h_scr[...] = (y * a + mod_ref[0, 1, i_shift:i_shift + 1, :]).astype(BF16)

    @pl.when(row0 < n_ctx)
    def _():
        shift, scale = _row_mods(mod_ref, row0, x.shape[0], n_ctx, (i_shift, i_scale))
        h_scr[...] = ((y * g) * (1.0 + scale) + shift).astype(BF16)


def _silu(x):
    return x * jax.nn.sigmoid(x)


def _gelu_tanh(x):
    return 0.5 * x * (1.0 + jnp.tanh(math.sqrt(2.0 / math.pi) * (x + 0.044715 * (x * x * x))))


def _adaln_kernel(cond_ref, w_ref, b_ref, o_ref):
    s = _silu(cond_ref[...])
    o_ref[0] = jnp.dot(s.astype(BF16), w_ref[0].astype(BF16),
                       preferred_element_type=F32) + b_ref[0]


def _adaln(cond8, ada_w, ada_b):
    depth, d, n6 = ada_w.shape
    tn = _pick_tile(n6, 1024, V7X_LANES)
    return pl.pallas_call(
        _adaln_kernel,
        grid=(depth, n6 // tn),
        in_specs=[pl.BlockSpec((8, d), lambda l, j: (0, 0)),
                  pl.BlockSpec((1, d, tn), lambda l, j: (l, 0, j)),
                  pl.BlockSpec((1, 1, tn), lambda l, j: (l, 0, j))],
        out_specs=pl.BlockSpec((1, 8, tn), lambda l, j: (l, 0, j)),
        out_shape=jax.ShapeDtypeStruct((depth, 8, n6), F32),
        compiler_params=_cparams(("parallel", "parallel")),
        name="adaln",
    )(cond8, ada_w, ada_b.reshape(depth, 1, n6))


def _proj_kernel(x_ref, mod_ref, g_ref, w_ref, o_ref, h_scr, *, n_ctx, tm):
    row0 = pl.program_id(1) * tm

    @pl.when(pl.program_id(2) == 0)
    def _():
        _norm_mod_into(h_scr, x_ref[0], g_ref[...], mod_ref, row0, n_ctx, 0, 1)

    o_ref[0] = jnp.dot(h_scr[...], w_ref[...], preferred_element_type=F32).astype(o_ref.dtype)


def _proj(xc, mods, g, w, n_ctx):
    b, n, d = xc.shape
    nout = w.shape[1]
    tm = _pick_tile(n, 1280, 128)
    tn = _pick_tile(nout, 1024, V7X_LANES)
    return pl.pallas_call(
        functools.partial(_proj_kernel, n_ctx=n_ctx, tm=tm),
        grid=(b, n // tm, nout // tn),
        in_specs=[pl.BlockSpec((1, tm, d), lambda bi, i, j: (bi, i, 0)),
                  pl.BlockSpec((1, 2, 6, d), lambda bi, i, j: (bi, 0, 0, 0)),
                  pl.BlockSpec((1, d), lambda bi, i, j: (0, 0)),
                  pl.BlockSpec((d, tn), lambda bi, i, j: (0, j))],
        out_specs=pl.BlockSpec((1, tm, tn), lambda bi, i, j: (bi, i, j)),
        out_shape=jax.ShapeDtypeStruct((b, n, nout), BF16),
        scratch_shapes=[pltpu.VMEM((tm, d), BF16)],
        compiler_params=_cparams(("parallel", "parallel", "arbitrary")),
        name="proj",
    )(xc, mods, g.reshape(1, d), w)


def _qk_prep_kernel(q_ref, k_ref, cos_ref, sin_ref, gq_ref, gk_ref, grp_ref, qo_ref, ko_ref, *, heads, q_scale):
    cos = cos_ref[...]
    sin = sin_ref[...]
    w2 = 2 * V_DIM
    for src, g_ref, dst, sc in ((q_ref, gq_ref, qo_ref, q_scale), (k_ref, gk_ref, ko_ref, 1.0)):
        g = jnp.concatenate([g_ref[...], g_ref[...]], axis=1)
        for hp in range(heads // 2):
            x = src[0, :, hp * w2:(hp + 1) * w2].astype(F32)
            t = x * x
            t_hi = t.astype(BF16)
            t_lo = (t - t_hi.astype(F32)).astype(BF16)
            ss = (jnp.dot(t_hi, grp_ref[...], preferred_element_type=F32)
                  + jnp.dot(t_lo, grp_ref[...], preferred_element_type=F32))
            y = x * lax.rsqrt(ss * (1.0 / HEAD_DIM) + EPS) * g
            for hh in range(2):
                yh = y[:, hh * V_DIM:(hh + 1) * V_DIM]
                yh = yh * cos + pltpu.roll(yh, 64, axis=1) * sin
                c0 = hp * w2 + hh * V_DIM
                dst[0, :, c0:c0 + V_DIM] = (yh * sc).astype(BF16)


def _qk_prep(u, cos_t, sin_t, gq, gk, heads, q_col, k_col):
    b, n, _ = u.shape
    dq = heads * V_DIM
    assert heads % 2 == 0
    tm = _pick_tile(n, 640, 128)
    out = jax.ShapeDtypeStruct((b, n, dq), BF16)
    lane = np.arange(2 * V_DIM)
    group = ((lane[:, None] // V_DIM == lane[None, :] // V_DIM)
             & (lane[:, None] % 2 == lane[None, :] % 2)).astype(np.float32)
    return pl.pallas_call(
        functools.partial(_qk_prep_kernel, heads=heads, q_scale=Q_SCALE),
        grid=(b, n // tm),
        in_specs=[pl.BlockSpec((1, tm, dq), lambda bi, i: (bi, i, q_col)),
                  pl.BlockSpec((1, tm, dq), lambda bi, i: (bi, i, k_col)),
                  pl.BlockSpec((tm, V_DIM), lambda bi, i: (i, 0)),
                  pl.BlockSpec((tm, V_DIM), lambda bi, i: (i, 0)),
                  pl.BlockSpec((1, V_DIM), lambda bi, i: (0, 0)),
                  pl.BlockSpec((1, V_DIM), lambda bi, i: (0, 0)),
                  pl.BlockSpec((2 * V_DIM, 2 * V_DIM), lambda bi, i: (0, 0))],
        out_specs=[pl.BlockSpec((1, tm, dq), lambda bi, i: (bi, i, 0)),
                   pl.BlockSpec((1, tm, dq), lambda bi, i: (bi, i, 0))],
        out_shape=[out, out],
        compiler_params=_cparams(("parallel", "parallel")),
        name="qk_prep",
    )(u, u, cos_t, sin_t, gq, gk, jnp.asarray(group, BF16))


def _attn_kernel(lamv_ref, hg_ref, q_ref, k_ref, vt_ref, o_ref, m_scr, l_scr, acc_scr, cm_scr, alpha_scr,
                 s_scr, p_scr,
                 *, n_ctx, tk, n_lat_chunks, lam_init, rb):
    q = q_ref[0]
    tq = q.shape[0]
    lane = lax.broadcasted_iota(jnp.int32, q.shape, 1)
    even = (lane % 2) == 0
    zero = jnp.zeros_like(q)
    qm = (jnp.where(even, q, zero), jnp.where(even, zero, q))

    m_scr[...] = jnp.full(m_scr.shape, NEG_BIG, F32)
    l_scr[...] = jnp.zeros(l_scr.shape, F32)
    acc_scr[...] = jnp.zeros(acc_scr.shape, F32)

    def scores(kc, sb):
        rows = kc.shape[0]
        for mp in range(2):
            s = lax.dot_general(kc, qm[mp], (((1,), (1,)), ((), ())), preferred_element_type=F32)
            s_scr[sb, mp, 0:rows, :] = s
            cm_scr[sb, mp] = jnp.broadcast_to(jnp.max(s, axis=0, keepdims=True), (8, tq))

    def softmax(sb, rows):
        for mp in range(2):
            m_old = m_scr[mp]
            m_new = jnp.maximum(m_old, cm_scr[sb, mp])
            m_scr[mp] = m_new
            alpha = jnp.exp2(m_old - m_new)
            alpha_scr[sb, mp] = alpha
            lsum = jnp.zeros((8, tq), F32)
            for r0 in range(0, rows, rb):
                p = jnp.exp2(s_scr[sb, mp, r0:r0 + rb, :] - m_new[0:1, :])
                lsum = lsum + jnp.sum(p.reshape(rb // 8, 8, tq), axis=0)
                p_scr[sb, mp, r0:r0 + rb, :] = p.astype(BF16)
            l_scr[mp] = alpha * l_scr[mp] + jnp.broadcast_to(jnp.sum(lsum, axis=0, keepdims=True), (8, tq))

    def values(vtc, sb):
        rows = vtc.shape[1]
        for mp in range(2):
            acc_scr[mp] = alpha_scr[sb, mp, 0:1, :] * acc_scr[mp] + jnp.dot(
                vtc, p_scr[sb, mp, 0:rows, :], preferred_element_type=F32)

    units = tk // n_ctx

    def k_chunk(j):
        return k_ref[0, pl.ds(pl.multiple_of(n_ctx + (j - 1) * tk, n_ctx), tk), :]

    def vt_chunk(j):
        return jnp.concatenate([vt_ref[0, 0, 1 + (j - 1) * units + u] for u in range(units)], axis=1)

    n_chunks = n_lat_chunks + 1
    scores(k_ref[0, 0:n_ctx, :], 0)
    if n_chunks == 1:
        softmax(0, n_ctx)
        values(vt_ref[0, 0, 0], 0)
    else:
        assert n_chunks % 2 == 1
        scores(k_chunk(1), 1)
        softmax(0, n_ctx)
        scores(k_chunk(2), 0)
        softmax(1, tk)
        values(vt_ref[0, 0, 0], 0)

        def two_steps(i, carry):
            t = 2 * i + 1
            softmax(0, tk)
            scores(k_chunk(t + 2), 1)
            values(vt_chunk(t), 1)
            softmax(1, tk)
            scores(k_chunk(t + 3), 0)
            values(vt_chunk(t + 1), 0)
            return carry

        lax.fori_loop(0, (n_chunks - 3) // 2, two_steps, 0)
        softmax(0, tk)
        values(vt_chunk(n_chunks - 2), 1)
        values(vt_chunk(n_chunks - 1), 0)

    lv = lamv_ref[...]
    lam = (jnp.exp(jnp.sum(lv[0:1] * lv[1:2], axis=1, keepdims=True))
           - jnp.exp(jnp.sum(lv[2:3] * lv[3:4], axis=1, keepdims=True)) + lam_init)
    ot = acc_scr[0] / l_scr[0, 0:1, :] - lam * (acc_scr[1] / l_scr[1, 0:1, :])
    o = ot.T
    o = o * lax.rsqrt(jnp.mean(o * o, axis=-1, keepdims=True) + EPS)
    o_ref[0] = (o * hg_ref[...] * (1.0 - lam_init)).astype(o_ref.dtype)


def _attn_fast_kernel(lamv_ref, hg_ref, q_ref, k_ref, vt_ref, o_ref, l_scr, acc_scr, p_scr,
                      *, n_ctx, tk, n_lat_chunks, lam_init, rb, unroll):
    q = q_ref[0]
    tq = q.shape[0]
    lane = lax.broadcasted_iota(jnp.int32, q.shape, 1)
    even = (lane % 2) == 0
    zero = jnp.zeros_like(q)
    qm = (jnp.where(even, q, zero), jnp.where(even, zero, q))
    nt = (((1,), (1,)), ((), ()))
    ref = lamv_ref[4:5, 0:1]

    l_scr[...] = jnp.zeros(l_scr.shape, F32)
    acc_scr[...] = jnp.zeros(acc_scr.shape, F32)

    def scores_softmax(kc, pb):
        rows = kc.shape[0]
        for mp in range(2):
            lsum = jnp.zeros((8, tq), F32)
            for r0 in range(0, rows, rb):
                s = lax.dot_general(kc[r0:r0 + rb, :], qm[mp], nt, preferred_element_type=F32)
                p = jnp.exp2(s - ref)
                lsum = lsum + jnp.sum(p.reshape(rb // 8, 8, tq), axis=0)
                p_scr[pb, mp, r0:r0 + rb, :] = p.astype(BF16)
            l_scr[mp] += lsum

    def values(vtc, pb):
        rows = vtc.shape[1]
        for mp in range(2):
            acc_scr[mp] += jnp.dot(vtc, p_scr[pb, mp, 0:rows, :], preferred_element_type=F32)

    units = tk // n_ctx

    def k_chunk(j):
        return k_ref[0, pl.ds(pl.multiple_of(n_ctx + (j - 1) * tk, n_ctx), tk), :]

    def vt_chunk(j):
        return jnp.concatenate([vt_ref[0, 0, 1 + (j - 1) * units + u] for u in range(units)], axis=1)

    scores_softmax(k_ref[0, 0:n_ctx, :], 0)
    if n_lat_chunks == 0:
        values(vt_ref[0, 0, 0], 0)
    else:
        assert n_lat_chunks % 2 == 0
        scores_softmax(k_chunk(1), 1)
        values(vt_ref[0, 0, 0], 0)

        def two_steps(t):
            scores_softmax(k_chunk(t), 0)
            values(vt_chunk(t - 1), 1)
            scores_softmax(k_chunk(t + 1), 1)
            values(vt_chunk(t), 0)

        n_pairs = (n_lat_chunks - 2) // 2
        n_trips = n_pairs // unroll

        def trip(i, carry):
            for u in range(unroll):
                two_steps(2 + 2 * (i * unroll + u))
            return carry

        if n_trips:
            lax.fori_loop(0, n_trips, trip, 0)
        for pr in range(n_trips * unroll, n_pairs):
            two_steps(2 + 2 * pr)
        scores_softmax(k_chunk(n_lat_chunks), 0)
        values(vt_chunk(n_lat_chunks - 1), 1)
        values(vt_chunk(n_lat_chunks), 0)

    lv = lamv_ref[...]
    lam = (jnp.exp(jnp.sum(lv[0:1] * lv[1:2], axis=1, keepdims=True))
           - jnp.exp(jnp.sum(lv[2:3] * lv[3:4], axis=1, keepdims=True)) + lam_init)
    l0 = jnp.sum(l_scr[0], axis=0, keepdims=True)
    l1 = jnp.sum(l_scr[1], axis=0, keepdims=True)
    ot = acc_scr[0] / l0 - lam * (acc_scr[1] / l1)
    o = ot.T
    o = o * lax.rsqrt(jnp.mean(o * o, axis=-1, keepdims=True) + EPS)
    o_ref[0] = (o * hg_ref[...] * (1.0 - lam_init)).astype(o_ref.dtype)


def _with_prev(kernel_fn, n_in):
    def wrapped(*refs, **kw):
        kernel_fn(*refs[:n_in], *refs[n_in + 1:], **kw)
    return wrapped


def _attention_calls(kernel_fn, lead, lead_specs, scratch_fn, q, k, vt, n_ctx, kw, name):
    b, n, dq = q.shape
    heads = dq // V_DIM
    tq = n_ctx
    s = n - n_ctx
    n_in = len(lead) + 3
    o = pl.pallas_call(
        functools.partial(kernel_fn, n_lat_chunks=s // kw["tk"], **kw),
        grid=(b, heads, s // tq),
        in_specs=lead_specs + [
            pl.BlockSpec((1, tq, V_DIM), lambda bi, h, i: (bi, i + 1, h)),
            pl.BlockSpec((1, n, V_DIM), lambda bi, h, i: (bi, 0, h)),
            pl.BlockSpec((1, 1, n // n_ctx, V_DIM, n_ctx), lambda bi, h, i: (bi, h, 0, 0, 0))],
        out_specs=pl.BlockSpec((1, tq, V_DIM), lambda bi, h, i: (bi, i + 1, h)),
        out_shape=jax.ShapeDtypeStruct((b, n, dq), BF16),
        scratch_shapes=scratch_fn(tq),
        compiler_params=_cparams(("parallel", "parallel", "arbitrary")),
        name=name,
    )(*lead, q, k, vt)
    return pl.pallas_call(
        functools.partial(_with_prev(kernel_fn, n_in), n_lat_chunks=0, **kw),
        grid=(b, heads, 1),
        in_specs=lead_specs + [
            pl.BlockSpec((1, tq, V_DIM), lambda bi, h, i: (bi, 0, h)),
            pl.BlockSpec((1, n_ctx, V_DIM), lambda bi, h, i: (bi, 0, h)),
            pl.BlockSpec((1, 1, 1, V_DIM, n_ctx), lambda bi, h, i: (bi, h, 0, 0, 0)),
            pl.BlockSpec(memory_space=pl.ANY)],
        out_specs=pl.BlockSpec((1, tq, V_DIM), lambda bi, h, i: (bi, 0, h)),
        out_shape=jax.ShapeDtypeStruct((b, n, dq), BF16),
        scratch_shapes=scratch_fn(tq),
        input_output_aliases={n_in: 0},
        compiler_params=_cparams(("parallel", "parallel", "arbitrary")),
        name=name + "_ctx",
    )(*lead, q, k, vt, o)


def _attention(lam_vecs, head_g, q, k, vt, n_ctx, lam_init, bound):
    b, n, dq = q.shape
    s = n - n_ctx
    tk = _pick_tile(s // 2, 1024, n_ctx)
    rb = _pick_tile(n_ctx, 128, 8)
    kw = dict(n_ctx=n_ctx, tk=tk, lam_init=lam_init, rb=rb)
    lamv = jnp.concatenate([lam_vecs, jnp.full((4, HEAD_DIM), bound, F32)], axis=0)
    spec2 = [pl.BlockSpec((8, HEAD_DIM), lambda bi, h, i: (0, 0)),
             pl.BlockSpec((1, V_DIM), lambda bi, h, i: (0, 0))]

    def robust_scratch(tq):
        return [pltpu.VMEM((2, 8, tq), F32), pltpu.VMEM((2, 8, tq), F32),
                pltpu.VMEM((2, V_DIM, tq), F32), pltpu.VMEM((2, 2, 8, tq), F32),
                pltpu.VMEM((2, 2, 8, tq), F32), pltpu.VMEM((2, 2, tk, tq), F32),
                pltpu.VMEM((2, 2, tk, tq), BF16)]

    def fast_scratch(tq):
        return [pltpu.VMEM((2, 8, tq), F32), pltpu.VMEM((2, V_DIM, tq), F32),
                pltpu.VMEM((2, 2, tk, tq), BF16)]

    def fast(args):
        return _attention_calls(_attn_fast_kernel, [lamv, head_g], spec2, fast_scratch,
                                *args, n_ctx, dict(kw, unroll=FAST_ATTN_UNROLL), "diff_attn")

    def robust(args):
        return _attention_calls(_attn_kernel, [lamv, head_g], spec2, robust_scratch,
                                *args, n_ctx, kw, "diff_attn_online")

    return lax.cond(bound <= SAFE_SCORE_BOUND, fast, robust, (q, k, vt))


def _dft_cos_sin(n, rows=None, cols=None, mod=None):
    mod = n if mod is None else mod
    r = jnp.arange(n if rows is None else rows, dtype=jnp.int32)[:, None]
    c = jnp.arange(n if cols is None else cols, dtype=jnp.int32)[None, :]
    ang = ((r * c) % mod).astype(F32) * (2.0 * math.pi / mod)
    return jnp.cos(ang), jnp.sin(ang)


def _channel_dft(x, cs_ref, groups):
    cg = x.shape[1] // groups
    wr, wi = [], []
    for g in range(groups):
        w = jnp.dot(x[:, g * cg:(g + 1) * cg], cs_ref[...], preferred_element_type=F32)
        wr.append(w[:, :cg])
        wi.append(w[:, cg:])
    return jnp.concatenate(wr, axis=1), jnp.concatenate(wi, axis=1)


def _fourier1_kernel(x_ref, cs_ref, kc_ref, ks_ref, tc_ref, ts_ref, zr_ref, zi_ref, *, groups):
    ls, tf, c = x_ref.shape[1:]
    x = x_ref[0].reshape(ls * tf, c)
    wr, wi = _channel_dft(x, cs_ref, groups)
    wr = wr.astype(BF16)
    wi = wi.astype(BF16)
    kc = kc_ref[...]
    ks = ks_ref[...]
    zr = jnp.dot(kc, wr, preferred_element_type=F32) + jnp.dot(ks, wi, preferred_element_type=F32)
    zi = jnp.dot(kc, wi, preferred_element_type=F32) - jnp.dot(ks, wr, preferred_element_type=F32)
    reps = c // V7X_LANES
    tc = jnp.concatenate([tc_ref[0]] * reps, axis=1)
    ts = jnp.concatenate([ts_ref[0]] * reps, axis=1)
    zr_ref[0] = (zr * tc + zi * ts).astype(BF16).reshape(ls, tf, c)
    zi_ref[0] = (zi * tc - zr * ts).astype(BF16).reshape(ls, tf, c)


def _fourier2_kernel(zr_ref, zi_ref, fc_ref, fs_ref, y_ref, *, scale):
    for u in range(zr_ref.shape[1]):
        y = (jnp.dot(fc_ref[...], zr_ref[0, u], preferred_element_type=F32)
             + jnp.dot(fs_ref[...], zi_ref[0, u], preferred_element_type=F32))
        y_ref[0, u] = (y * scale).astype(BF16)


def _fourier_ctx_kernel(x_ref, cs_ref, pc_ref, ps_ref, y_ref, *, groups, scale):
    wr, wi = _channel_dft(x_ref[0], cs_ref, groups)
    y = (jnp.dot(pc_ref[...], wr.astype(BF16), preferred_element_type=F32)
         + jnp.dot(ps_ref[...], wi.astype(BF16), preferred_element_type=F32))
    y_ref[0] = (y * scale).astype(BF16)


def _fourier_tables(s, c):
    cg = c // N_FOURIER_GROUPS
    cc, sc = _dft_cos_sin(cg)
    cs = jnp.concatenate([cc, -sc], axis=1).astype(BF16)
    return cg, cs


def _fourier_latent(fx, groups):
    b, s, c = fx.shape
    cg, cs = _fourier_tables(s, c)
    ls = FOURIER_SLOW
    lf = s // ls
    tf = V7X_BF16_SUBLANES
    assert ls * lf == s and lf % tf == 0
    rows = ls * tf
    pc, ps = _dft_cos_sin(ls)
    eye = jnp.eye(tf, dtype=F32)
    kc = jnp.kron(pc, eye).astype(BF16)
    ks = jnp.kron(ps, eye).astype(BF16)
    u_idx = jnp.repeat(jnp.arange(ls, dtype=jnp.int32), tf)[None, :]
    f_idx = (jnp.arange(lf // tf, dtype=jnp.int32)[:, None] * tf
             + jnp.tile(jnp.arange(tf, dtype=jnp.int32), ls)[None, :])
    ang = ((u_idx * f_idx) % s).astype(F32) * (2.0 * math.pi / s)
    tc = jnp.broadcast_to(jnp.cos(ang)[:, :, None], (lf // tf, rows, V7X_LANES))
    ts = jnp.broadcast_to(jnp.sin(ang)[:, :, None], (lf // tf, rows, V7X_LANES))

    x4 = fx.reshape(b, ls, lf, c)
    zshape = jax.ShapeDtypeStruct((b, ls, lf, c), BF16)
    blk = pl.BlockSpec((1, ls, tf, c), lambda bi, i: (bi, 0, i, 0))
    zr, zi = pl.pallas_call(
        functools.partial(_fourier1_kernel, groups=groups),
        grid=(b, lf // tf),
        in_specs=[blk,
                  pl.BlockSpec((cg, 2 * cg), lambda bi, i: (0, 0)),
                  pl.BlockSpec((rows, rows), lambda bi, i: (0, 0)),
                  pl.BlockSpec((rows, rows), lambda bi, i: (0, 0)),
                  pl.BlockSpec((1, rows, V7X_LANES), lambda bi, i: (i, 0, 0)),
                  pl.BlockSpec((1, rows, V7X_LANES), lambda bi, i: (i, 0, 0))],
        out_specs=[blk, blk],
        out_shape=[zshape, zshape],
        compiler_params=_cparams(("parallel", "parallel")),
        name="fourier_slow",
    )(x4, cs, kc, ks, tc, ts)

    fc, fs = _dft_cos_sin(lf)
    tu = _pick_tile(ls, 4, 1)
    blk2 = pl.BlockSpec((1, tu, lf, c), lambda bi, i: (bi, i, 0, 0))
    yt = pl.pallas_call(
        functools.partial(_fourier2_kernel, scale=1.0 / math.sqrt(s * cg)),
        grid=(b, ls // tu),
        in_specs=[blk2, blk2,
                  pl.BlockSpec((lf, lf), lambda bi, i: (0, 0)),
                  pl.BlockSpec((lf, lf), lambda bi, i: (0, 0))],
        out_specs=blk2,
        out_shape=zshape,
        compiler_params=_cparams(("parallel", "parallel")),
        name="fourier_fast",
    )(zr, zi, fc.astype(BF16), fs.astype(BF16))
    return jnp.swapaxes(yt, 1, 2).reshape(b, s, c)


def _fourier_ctx(fxc, groups):
    b, n_ctx, c = fxc.shape
    cg, cs = _fourier_tables(n_ctx, c)
    pc, ps = _dft_cos_sin(n_ctx)
    return pl.pallas_call(
        functools.partial(_fourier_ctx_kernel, groups=groups, scale=1.0 / math.sqrt(n_ctx * cg)),
        grid=(b,),
        in_specs=[pl.BlockSpec((1, n_ctx, c), lambda bi: (bi, 0, 0)),
                  pl.BlockSpec((cg, 2 * cg), lambda bi: (0, 0)),
                  pl.BlockSpec((n_ctx, n_ctx), lambda bi: (0, 0)),
                  pl.BlockSpec((n_ctx, n_ctx), lambda bi: (0, 0))],
        out_specs=pl.BlockSpec((1, n_ctx, c), lambda bi: (bi, 0, 0)),
        out_shape=jax.ShapeDtypeStruct((b, n_ctx, c), BF16),
        compiler_params=_cparams(("parallel",)),
        name="fourier_ctx",
    )(fxc, cs, pc.astype(BF16), ps.astype(BF16))


def _out_ab_kernel(x_ref, f_ref, o_ref_in, mod_ref, w_ref, out_ref, *, n_ctx, tm, c):
    (gate,) = _row_mods(mod_ref, pl.program_id(1) * tm, tm, n_ctx, (2,))
    y = (jnp.dot(f_ref[0], w_ref[0:c, :], preferred_element_type=F32)
         + jnp.dot(o_ref_in[0], w_ref[c:, :], preferred_element_type=F32))
    out_ref[0] = x_ref[0] + gate * y


def _out_ab(xc, f, o, mods, w_out, n_ctx):
    b, n, d = xc.shape
    c = f.shape[2]
    tm = _pick_tile(n, 832, 64)
    return pl.pallas_call(
        functools.partial(_out_ab_kernel, n_ctx=n_ctx, tm=tm, c=c),
        grid=(b, n // tm),
        in_specs=[pl.BlockSpec((1, tm, d), lambda bi, i: (bi, i, 0)),
                  pl.BlockSpec((1, tm, c), lambda bi, i: (bi, i, 0)),
                  pl.BlockSpec((1, tm, o.shape[2]), lambda bi, i: (bi, i, 0)),
                  pl.BlockSpec((1, 2, 6, d), lambda bi, i: (bi, 0, 0, 0)),
                  pl.BlockSpec(w_out.shape, lambda bi, i: (0, 0), pipeline_mode=pl.Buffered(1))],
        out_specs=pl.BlockSpec((1, tm, d), lambda bi, i: (bi, i, 0)),
        out_shape=jax.ShapeDtypeStruct((b, n, d), F32),
        compiler_params=_cparams(("parallel", "parallel")),
        name="out_proj_attn",
    )(xc, f, o, mods, w_out)


def _out_lru_kernel(x_ref, yf_ref, yr_ref, gate_ref, mod_ref, w_ref, out_ref, *, n_ctx, tm):
    (gate,) = _row_mods(mod_ref, pl.program_id(1) * tm, tm, n_ctx, (2,))
    z = (yf_ref[0].astype(F32) + yr_ref[0].astype(F32)) * _gelu_tanh(gate_ref[0].astype(F32))
    y = jnp.dot(z.astype(BF16), w_ref[...], preferred_element_type=F32)
    out_ref[0] = x_ref[0] + gate * y


def _out_lru(xc, yf, yr, u, mods, w_out, n_ctx):
    b, n, d = xc.shape
    tm = _pick_tile(n, 640, 64)
    row = pl.BlockSpec((1, tm, d), lambda bi, i: (bi, i, 0))
    return pl.pallas_call(
        functools.partial(_out_lru_kernel, n_ctx=n_ctx, tm=tm),
        grid=(b, n // tm),
        in_specs=[row, row, row, row,
                  pl.BlockSpec((1, 2, 6, d), lambda bi, i: (bi, 0, 0, 0)),
                  pl.BlockSpec(w_out.shape, lambda bi, i: (0, 0), pipeline_mode=pl.Buffered(1))],
        out_specs=row,
        out_shape=jax.ShapeDtypeStruct((b, n, d), F32),
        compiler_params=_cparams(("parallel", "parallel")),
        name="out_proj_lru",
    )(xc, yf, yr, u, mods, w_out)


def _conv_kernel(main_ref, prev_ref, next_ref, w_ref, b_ref, o_ref, ext_scr, *, n_blocks, halo):
    i = pl.program_id(1)
    tb = main_ref.shape[1]
    first = jnp.logical_or(i == 0, i == 1)
    last = jnp.logical_or(i == 0, i == n_blocks - 1)
    w = w_ref[...]
    bias = b_ref[...]
    zero = jnp.minimum(i, 0)
    for j in range(w.shape[1] // V7X_LANES):
        cols = slice(j * V7X_LANES, (j + 1) * V7X_LANES)
        ext_scr[j, 0:halo, :] = jnp.where(first, 0.0, prev_ref[0, :, cols].astype(F32))
        ext_scr[j, halo:halo + tb, :] = main_ref[0, :, cols].astype(F32)
        ext_scr[j, halo + tb:, :] = jnp.where(last, 0.0, next_ref[0, :, cols].astype(F32))
        rs = min(tb, 64)
        for r0 in range(0, tb, rs):
            acc = jnp.broadcast_to(bias[:, cols], (rs, V7X_LANES))
            for k in range(w.shape[0]):
                acc = acc + w[k:k + 1, cols] * ext_scr[j, pl.ds(halo + r0 + k - 2 + zero, rs), :]
            o_ref[0, r0:r0 + rs, cols] = acc


def _conv(u, conv_w, conv_b, n_ctx, col):
    b, n, d2 = u.shape
    d = d2 // 2
    tb = n_ctx
    halo = V7X_BF16_SUBLANES
    nb = n // tb
    hb = tb // halo
    return pl.pallas_call(
        functools.partial(_conv_kernel, n_blocks=nb, halo=halo),
        grid=(b, nb),
        in_specs=[pl.BlockSpec((1, tb, d), lambda bi, i: (bi, i, col)),
                  pl.BlockSpec((1, halo, d), lambda bi, i: (bi, jnp.maximum(i * hb - 1, 0), col)),
                  pl.BlockSpec((1, halo, d), lambda bi, i: (bi, jnp.minimum((i + 1) * hb, n // halo - 1), col)),
                  pl.BlockSpec(conv_w.shape, lambda bi, i: (0, 0)),
                  pl.BlockSpec((1, d), lambda bi, i: (0, 0))],
        out_specs=pl.BlockSpec((1, tb, d), lambda bi, i: (bi, i, 0)),
        out_shape=jax.ShapeDtypeStruct((b, n, d), F32),
        scratch_shapes=[pltpu.VMEM((d // V7X_LANES, tb + 2 * halo, V7X_LANES), F32)],
        compiler_params=_cparams(("parallel", "parallel")),
        name="lru_conv",
    )(u, u, u, conv_w, conv_b.reshape(1, d))


def _lru_scan_kernel(xf_ref, xr_ref, gaw_ref, gab_ref, gxw_ref, gxb_ref, lam_ref, yf_ref, yr_ref,
                     h_scr, a0_scr, a1_scr, b0_scr, b1_scr, y0_scr, y1_scr):
    t = pl.program_id(1)
    tb, d = xf_ref.shape[1:]
    nblk = gaw_ref.shape[1]
    bs = d // nblk
    nl = d // V7X_LANES
    pitch = SCAN_ROW_PITCH
    a_scr, b_scr, y_scr = (a0_scr, a1_scr), (b0_scr, b1_scr), (y0_scr, y1_scr)

    @pl.when(t == 0)
    def _():
        h_scr[...] = jnp.zeros(h_scr.shape, F32)

    assert bs % V7X_LANES == 0
    for dr, x_ref in enumerate((xf_ref, xr_ref)):
        for n in range(nblk):
            blk = slice(n * bs, (n + 1) * bs)
            xc = x_ref[0, :, blk]
            xb = xc.astype(BF16)
            th_a = jnp.tanh(jnp.dot(xb, gaw_ref[dr, n], preferred_element_type=F32) + gab_ref[dr:dr + 1, blk])
            th_x = jnp.tanh(jnp.dot(xb, gxw_ref[dr, n], preferred_element_type=F32) + gxb_ref[dr:dr + 1, blk])
            z = -lam_ref[dr:dr + 1, blk]
            softplus = jnp.maximum(z, 0.0) + jnp.log1p(jnp.exp(-jnp.abs(z)))
            c = (-0.5 * LRU_C) * softplus
            a = jnp.exp(c + c * th_a)
            w = 1.0 - a * a
            mult = jnp.where(w > 0.0, w * lax.rsqrt(w), 0.0)
            bb = mult * (0.5 * xc) * (1.0 + th_x)
            for jj in range(bs // V7X_LANES):
                j = n * (bs // V7X_LANES) + jj
                cols = slice(jj * V7X_LANES, (jj + 1) * V7X_LANES)
                a_scr[dr][pl.ds(j, tb, stride=pitch), :] = a[:, cols]
                b_scr[dr][pl.ds(j, tb, stride=pitch), :] = bb[:, cols]

    def step(i, carry):
        hf, hr = carry
        rf = i * pitch
        hf = a_scr[0][pl.ds(rf, nl), :] * hf + b_scr[0][pl.ds(rf, nl), :]
        y_scr[0][pl.ds(rf, nl), :] = hf
        rr = (tb - 1 - i) * pitch
        hr = a_scr[1][pl.ds(rr, nl), :] * hr + b_scr[1][pl.ds(rr, nl), :]
        y_scr[1][pl.ds(rr, nl), :] = hr
        return hf, hr

    hf, hr = lax.fori_loop(0, tb, step, (h_scr[0], h_scr[1]), unroll=8)
    h_scr[0] = hf
    h_scr[1] = hr
    for j in range(nl):
        cols = slice(j * V7X_LANES, (j + 1) * V7X_LANES)
        yf_ref[0, :, cols] = y_scr[0][pl.ds(j, tb, stride=pitch), :].astype(yf_ref.dtype)
        yr_ref[0, :, cols] = y_scr[1][pl.ds(j, tb, stride=pitch), :].astype(yr_ref.dtype)


def _lru_scan(xcv, ga_w, ga_b, gx_w, gx_b, lam, n_ctx):
    b, n, d = xcv.shape
    tb = n_ctx
    nt = n // tb
    fwd = pl.BlockSpec((1, tb, d), lambda bi, t: (bi, t, 0))
    rev = pl.BlockSpec((1, tb, d), lambda bi, t: (bi, jnp.where(t == 0, 0, nt - t), 0))
    wspec = pl.BlockSpec(ga_w.shape, lambda bi, t: (0, 0, 0, 0))
    vspec = pl.BlockSpec((2, d), lambda bi, t: (0, 0))
    out = jax.ShapeDtypeStruct((b, n, d), BF16)
    return pl.pallas_call(
        _lru_scan_kernel,
        grid=(b, nt),
        in_specs=[fwd, rev, wspec, vspec, wspec, vspec, vspec],
        out_specs=[fwd, rev],
        out_shape=[out, out],
        scratch_shapes=[pltpu.VMEM((2, d // V7X_LANES, V7X_LANES), F32)]
        + [pltpu.VMEM((tb * SCAN_ROW_PITCH, V7X_LANES), F32)] * 6,
        compiler_params=_cparams(("parallel", "arbitrary")),
        name="lru_scan",
    )(xcv, xcv, ga_w, ga_b, gx_w, gx_b, lam)


def _ffn_kernel(x_ref, mod_ref, g_ref, wg_ref, wu_ref, wd_ref, o_ref, h_scr, *, n_ctx, tm):
    k = pl.program_id(2)
    row0 = pl.program_id(1) * tm

    @pl.when(k == 0)
    def _():
        _norm_mod_into(h_scr, x_ref[0], g_ref[...], mod_ref, row0, n_ctx, 3, 4)
        o_ref[0] = jnp.zeros(o_ref.shape[1:], F32)

    h = h_scr[...]
    act = _silu(jnp.dot(h, wg_ref[...], preferred_element_type=F32)) * jnp.dot(
        h, wu_ref[...], preferred_element_type=F32)
    o_ref[0] += jnp.dot(act.astype(BF16), wd_ref[...], preferred_element_type=F32)

    @pl.when(k == pl.num_programs(2) - 1)
    def _():
        (gate,) = _row_mods(mod_ref, row0, tm, n_ctx, (5,))
        o_ref[0] = x_ref[0] + gate * o_ref[0]


def _ffn(xc, mods, g, wg, wu, wd, n_ctx):
    b, n, d = xc.shape
    dff = wg.shape[1]
    tm = _pick_tile(n, 832, 64)
    tf = _pick_tile(dff, 512, V7X_LANES)
    return pl.pallas_call(
        functools.partial(_ffn_kernel, n_ctx=n_ctx, tm=tm),
        grid=(b, n // tm, dff // tf),
        in_specs=[pl.BlockSpec((1, tm, d), lambda bi, i, k: (bi, i, 0)),
                  pl.BlockSpec((1, 2, 6, d), lambda bi, i, k: (bi, 0, 0, 0)),
                  pl.BlockSpec((1, d), lambda bi, i, k: (0, 0)),
                  pl.BlockSpec((d, tf), lambda bi, i, k: (0, k)),
                  pl.BlockSpec((d, tf), lambda bi, i, k: (0, k)),
                  pl.BlockSpec((tf, d), lambda bi, i, k: (k, 0))],
        out_specs=pl.BlockSpec((1, tm, d), lambda bi, i, k: (bi, i, 0)),
        out_shape=jax.ShapeDtypeStruct((b, n, d), F32),
        scratch_shapes=[pltpu.VMEM((tm, d), BF16)],
        compiler_params=_cparams(("parallel", "parallel", "arbitrary")),
        name="ffn",
    )(xc, mods, g.reshape(1, d), wg, wu, wd)


def _permute_qk_cols(w, heads):
    d = w.shape[0]
    w5 = w.reshape(d, heads, 2, HEAD_DIM // 2, 2)
    return jnp.transpose(w5, (0, 1, 4, 3, 2)).reshape(d, heads * V_DIM)


def _qk_gain_index():
    gain_idx = np.zeros(V_DIM, np.int32)
    for half in range(2):
        for j in range(HEAD_DIM // 2):
            for mp in range(2):
                gain_idx[half * HEAD_DIM + 2 * j + mp] = 2 * j + half
    return gain_idx


def _rope_tables(n_ctx, s):
    n_rows = s // GRID_W
    rows = jnp.repeat(jnp.arange(n_rows, dtype=F32), GRID_W, total_repeat_length=s)
    cols = jnp.tile(jnp.arange(GRID_W, dtype=F32), n_rows)
    n_axis = HEAD_DIM // 4
    inv = ROPE_THETA ** (-jnp.arange(n_axis, dtype=F32) / n_axis)
    ang = jnp.concatenate([rows[:, None] * inv, cols[:, None] * inv], axis=-1)
    cos = jnp.concatenate([jnp.ones((n_ctx, HEAD_DIM // 2), F32), jnp.cos(ang)], axis=0)
    sin = jnp.concatenate([jnp.zeros((n_ctx, HEAD_DIM // 2), F32), jnp.sin(ang)], axis=0)
    cos_h = jnp.repeat(cos, 2, axis=1)
    sin_h = jnp.repeat(sin, 2, axis=1)
    return jnp.concatenate([cos_h, cos_h], axis=1), jnp.concatenate([-sin_h, sin_h], axis=1)


def kernel(x, c, ctx, c_ctx, ada_w, ada_b, norm_mix_g, norm_ffn_g, ffn_w_gate, ffn_w_up, ffn_w_down, ab_w_in, ab_w_out, ab_q_norm_g, ab_k_norm_g, ab_lam_q1, ab_lam_k1, ab_lam_q2, ab_lam_k2, ab_head_norm_g, lru_w_in, lru_w_out, lru_conv_w, lru_conv_b, lru_gate_a_w, lru_gate_a_b, lru_gate_x_w, lru_gate_x_b, lru_lambda):
    b, s, d = x.shape
    n_ctx = ctx.shape[1]
    depth = ada_w.shape[0]
    c_f = d // 2
    heads = (d // 2) // V_DIM
    dq = heads * V_DIM

    cond8 = jnp.zeros((8, d), F32).at[:b].set(c).at[b].set(c_ctx)
    ada = _adaln(cond8, ada_w, ada_b).reshape(depth, 8, 6, d)
    mods_all = jnp.stack([jnp.broadcast_to(ada[:, b][:, None], (depth, b, 6, d)), ada[:, :b]], axis=2)

    gain_idx = _qk_gain_index()
    cos_t, sin_t = _rope_tables(n_ctx, s)

    xc = jnp.concatenate([ctx, x], axis=1)

    for l in range(depth):
        j = l // 2
        mods = mods_all[l]
        if l % 2 == 0:
            lam_init = 0.8 - 0.6 * math.exp(-0.3 * l)
            w_in = ab_w_in[j]
            w_in = jnp.concatenate([w_in[:, :c_f],
                                    _permute_qk_cols(w_in[:, c_f:c_f + dq], heads),
                                    _permute_qk_cols(w_in[:, c_f + dq:c_f + 2 * dq], heads),
                                    w_in[:, c_f + 2 * dq:]], axis=1).astype(BF16)
            u = _proj(xc, mods, norm_mix_g[l], w_in, n_ctx)
            gq = jnp.take(ab_q_norm_g[j], gain_idx).reshape(1, V_DIM)
            gk = jnp.take(ab_k_norm_g[j], gain_idx).reshape(1, V_DIM)
            q, k = _qk_prep(u, cos_t, sin_t, gq, gk, heads, c_f // dq, c_f // dq + 1)
            v = u[:, :, c_f + 2 * dq:]
            vt = jnp.transpose(v.reshape(b, (n_ctx + s) // n_ctx, n_ctx, heads, V_DIM), (0, 3, 1, 4, 2))
            lamv = jnp.stack([ab_lam_q1[j], ab_lam_k1[j], ab_lam_q2[j], ab_lam_k2[j]])
            bound = (HEAD_DIM * Q_SCALE * BOUND_MARGIN
                     * jnp.max(jnp.abs(ab_q_norm_g[j])) * jnp.max(jnp.abs(ab_k_norm_g[j])))
            o = _attention(lamv, ab_head_norm_g[j].reshape(1, V_DIM), q, k, vt, n_ctx, lam_init, bound)
            f_ctx = _fourier_ctx(u[:, :n_ctx, :c_f], N_FOURIER_GROUPS)
            f_lat = _fourier_latent(u[:, n_ctx:, :c_f], N_FOURIER_GROUPS)
            f = jnp.concatenate([f_ctx, f_lat], axis=1)
            xc = _out_ab(xc, f, o, mods, ab_w_out[j].astype(BF16), n_ctx)
        else:
            u = _proj(xc, mods, norm_mix_g[l], lru_w_in[j].astype(BF16), n_ctx)
            xcv = _conv(u, lru_conv_w[j], lru_conv_b[j], n_ctx, 1)
            yf, yr = _lru_scan(xcv, (0.5 * lru_gate_a_w[j]).astype(BF16), 0.5 * lru_gate_a_b[j],
                               (0.5 * lru_gate_x_w[j]).astype(BF16), 0.5 * lru_gate_x_b[j],
                               lru_lambda[j], n_ctx)
            xc = _out_lru(xc, yf, yr, u, mods, lru_w_out[j].astype(BF16), n_ctx)
        xc = _ffn(xc, mods, norm_ffn_g[l], ffn_w_gate[l].astype(BF16), ffn_w_up[l].astype(BF16),
                  ffn_w_down[l].astype(BF16), n_ctx)
    return xc[:, n_ctx:]
```
